```python
import math
import jax, jax.numpy as jnp
from jax import lax
import numpy as np

D_MODEL = 2048
BATCH = 4
SEQ = 8192
DEPTH = 4

GDN_DK = 128
GDN_DV = 128
GDN_W = 3 * D_MODEL // 8
GDN_HEADS = GDN_W // GDN_DV
GDN_QK = GDN_HEADS * GDN_DK
ATT_HD = 64
ATT_W = 3 * D_MODEL // 8
ATT_HEADS = ATT_W // ATT_HD
CONV_CH = D_MODEL - GDN_W - ATT_W
MIX_W = CONV_CH + GDN_W + ATT_W

CONV_WIDTH = 31
SHORT_CONV = 4
GDN_CHUNK = 64

ROPE_THETA = 500000.0
ROPE_DIM = ATT_HD // 4
DIL_PATTERNS = ((128, 1), (512, 4), (2048, 16))
ATT_BLOCK = 128
NEG_INF = -1e30

IN_SPLITS = (
    2 * CONV_CH, CONV_CH,
    GDN_QK, GDN_QK, GDN_W, GDN_W, GDN_HEADS, GDN_HEADS,
    ATT_W, ATT_W, ATT_W, ATT_W,
)
IN_W = sum(IN_SPLITS)

kernel_name = "hymba_style_conformer_gdn_dilated_hybrid"


def rms_norm(x, w, eps=1e-6):
    xf = x.astype(jnp.float32)
    y = xf * lax.rsqrt(jnp.mean(xf * xf, axis=-1, keepdims=True) + eps)
    return (y * w.astype(jnp.float32)).astype(x.dtype)


def layer_norm(x, w, b, eps=1e-5):
    xf = x.astype(jnp.float32)
    mu = jnp.mean(xf, axis=-1, keepdims=True)
    var = jnp.mean(jnp.square(xf - mu), axis=-1, keepdims=True)
    y = (xf - mu) * lax.rsqrt(var + eps) * w.astype(jnp.float32) + b.astype(jnp.float32)
    return y.astype(x.dtype)


def l2_normalize(x, eps=1e-6):
    xf = x.astype(jnp.float32)
    return xf * lax.rsqrt(jnp.sum(xf * xf, axis=-1, keepdims=True) + eps)


def causal_dwconv(x, w):
    K, C = w.shape
    return lax.conv_general_dilated(
        x, w[:, None, :].astype(x.dtype), window_strides=(1,), padding=[(K - 1, 0)],
        dimension_numbers=("NWC", "WIO", "NWC"), feature_group_count=C)


def rope_tables(S):
    half = ROPE_DIM // 2
    inv = ROPE_THETA ** (-jnp.arange(half, dtype=jnp.float32) / half)
    ang = jnp.arange(S, dtype=jnp.float32)[:, None] * inv[None, :]
    return jnp.cos(ang), jnp.sin(ang)


def apply_partial_rope(x, cos, sin):
    half = ROPE_DIM // 2
    c, s = cos[None, :, None, :], sin[None, :, None, :]
    x1, x2, rest = x[..., :half], x[..., half:ROPE_DIM], x[..., ROPE_DIM:]
    return jnp.concatenate([x1 * c - x2 * s, x2 * c + x1 * s, rest], axis=-1)


def conformer_conv(u, dw_w, dw_b, ln_w, ln_b, pw_w):
    a, b = jnp.split(u, 2, axis=-1)
    h = a * jax.nn.sigmoid(b)
    h = causal_dwconv(h, dw_w) + dw_b.astype(h.dtype)
    h = layer_norm(h, ln_w, ln_b)
    h = jax.nn.silu(h)
    return h @ pw_w.astype(h.dtype)


def chunk_gated_delta_rule(q, k, v, g, beta):
    B, S, H, DK = q.shape
    DV = v.shape[-1]
    C = GDN_CHUNK
    N = S // C
    f32 = jnp.float32

    def chunks(t):
        return t.astype(f32).reshape(B, N, C, H, -1).transpose(0, 3, 1, 2, 4)

    def chunks_h(t):
        return t.astype(f32).reshape(B, N, C, H).transpose(0, 3, 1, 2)

    q, k, v = chunks(q), chunks(k), chunks(v)
    beta = chunks_h(beta)
    g = jnp.cumsum(chunks_h(g), axis=-1)

    causal = jnp.tril(jnp.ones((C, C), dtype=bool))
    strict = jnp.tril(jnp.ones((C, C), dtype=bool), -1)
    diff = g[..., :, None] - g[..., None, :]
    decay = jnp.where(causal, jnp.exp(jnp.where(causal, diff, 0.0)), 0.0)

    kk = jnp.einsum("bhncd,bhnmd->bhncm", k, k)
    lower = jnp.where(strict, beta[..., :, None] * kk * decay, 0.0)
    eye = jnp.eye(C, dtype=f32)
    T = lax.linalg.triangular_solve(eye + lower, jnp.broadcast_to(eye, lower.shape),
                                    left_side=True, lower=True, unit_diagonal=True)
    w_v = jnp.einsum("bhncm,bhnmd->bhncd", T, v * beta[..., None])
    w_k = jnp.einsum("bhncm,bhnmd->bhncd", T, k * (beta * jnp.exp(g))[..., None])
    qk = jnp.where(causal, jnp.einsum("bhncd,bhnmd->bhncm", q, k) * decay, 0.0)
    q_dec = q * jnp.exp(g)[..., None]
    k_dec = k * jnp.exp(g[..., -1:] - g)[..., None]
    g_last = jnp.exp(g[..., -1])

    def step(state, xs):
        qk_i, qd_i, wv_i, wk_i, kd_i, gl_i = xs
        v_new = wv_i - jnp.einsum("bhcd,bhde->bhce", wk_i, state)
        o_i = jnp.einsum("bhcd,bhde->bhce", qd_i, state) + jnp.einsum("bhcm,bhme->bhce", qk_i, v_new)
        state = state * gl_i[..., None, None] + jnp.einsum("bhcd,bhce->bhde", kd_i, v_new)
        return state, o_i

    xs = (jnp.moveaxis(qk, 2, 0), jnp.moveaxis(q_dec, 2, 0), jnp.moveaxis(w_v, 2, 0),
          jnp.moveaxis(w_k, 2, 0), jnp.moveaxis(k_dec, 2, 0), jnp.moveaxis(g_last, 2, 0))
    state0 = jnp.zeros((B, H, DK, DV), f32)
    _, o = lax.scan(step, state0, xs)
    return o.transpose(1, 0, 3, 2, 4).reshape(B, S, H, DV)


def gated_deltanet(q, k, v, z, beta_in, alpha_in, conv_w, a_log, dt_bias, norm_w):
    B, S, _ = q.shape
    qkv = jax.nn.silu(causal_dwconv(jnp.concatenate([q, k, v], axis=-1), conv_w))
    q, k, v = jnp.split(qkv, [GDN_QK, 2 * GDN_QK], axis=-1)
    q = l2_normalize(q.reshape(B, S, GDN_HEADS, GDN_DK)) * (GDN_DK ** -0.5)
    k = l2_normalize(k.reshape(B, S, GDN_HEADS, GDN_DK))
    v = v.reshape(B, S, GDN_HEADS, GDN_DV)
    beta = jax.nn.sigmoid(beta_in.astype(jnp.float32))
    g = -jnp.exp(a_log.astype(jnp.float32)) * jax.nn.softplus(
        alpha_in.astype(jnp.float32) + dt_bias.astype(jnp.float32))
    o = chunk_gated_delta_rule(q, k, v, g, beta).astype(z.dtype)
    o = rms_norm(o, norm_w) * jax.nn.silu(z.reshape(B, S, GDN_HEADS, GDN_DV))
    return o.reshape(B, S, GDN_W)


def strided_window_attention(q, k, v, span, dil):
    B, S, H, E = q.shape
    unit = dil * ATT_BLOCK
    Sp = -(-S // unit) * unit
    Lr = Sp // dil
    nb = Lr // ATT_BLOCK

    def to_strided(t):
        t = jnp.pad(t, ((0, 0), (0, Sp - S), (0, 0), (0, 0)))
        return t.reshape(B, Lr, dil, H, E).transpose(0, 2, 1, 3, 4).reshape(B, dil, nb, ATT_BLOCK, H, E)

    def with_prev(t):
        prev = jnp.pad(t, ((0, 0), (0, 0), (1, 0), (0, 0), (0, 0), (0, 0)))[:, :, :-1]
        return jnp.concatenate([prev, t], axis=3)

    qs = to_strided(q)
    kb, vb = with_prev(to_strided(k)), with_prev(to_strided(v))
    s = jnp.einsum("bdnqhe,bdnkhe->bdnhqk", qs, kb) * (E ** -0.5)
    qi = jnp.arange(ATT_BLOCK)[:, None]
    ki = jnp.arange(2 * ATT_BLOCK)[None, :]
    dist = qi + ATT_BLOCK - ki
    blk = jnp.arange(nb)[:, None, None]
    valid = (dist >= 0) & (dist <= span) & ((blk - 1) * ATT_BLOCK + ki >= 0)
    s = jnp.where(valid[:, None], s, NEG_INF)
    m = jnp.max(s, axis=-1, keepdims=True)
    p = jnp.exp(s - m)
    l = jnp.sum(p, axis=-1, keepdims=True)
    o = jnp.einsum("bdnhqk,bdnkhe->bdnqhe", p / l, vb)
    lse = (m + jnp.log(l))[..., 0]
    o = o.reshape(B, dil, Lr, H, E).transpose(0, 2, 1, 3, 4).reshape(B, Sp, H, E)[:, :S]
    lse = lse.transpose(0, 1, 2, 4, 3).reshape(B, dil, Lr, H).transpose(0, 2, 1, 3).reshape(B, Sp, H)[:, :S]
    return o, lse


def dilated_attention(q, k, v, cos, sin):
    B, S, _ = q.shape
    dt = q.dtype
    q = apply_partial_rope(q.astype(jnp.float32).reshape(B, S, ATT_HEADS, ATT_HD), cos, sin)
    k = apply_partial_rope(k.astype(jnp.float32).reshape(B, S, ATT_HEADS, ATT_HD), cos, sin)
    v = v.astype(jnp.float32).reshape(B, S, ATT_HEADS, ATT_HD)
    outs, lses = [], []
    for window, dil in DIL_PATTERNS:
        o_g, lse_g = strided_window_attention(q, k, v, window // dil, dil)
        outs.append(o_g)
        lses.append(lse_g)
    wts = jax.nn.softmax(jnp.stack(lses, axis=0), axis=0)
    o = jnp.einsum("pbsh,pbshe->bshe", wts, jnp.stack(outs, axis=0))
    return o.reshape(B, S, ATT_W).astype(dt)


def setup_inputs(seed: int = 0) -> dict:
    key = jax.random.key(seed)
    ks = jax.random.split(key, 16)
    f32 = jnp.float32

    def nrm(k, shape, scale):
        return jax.random.normal(k, shape, f32) * scale

    x = nrm(ks[0], (BATCH, SEQ, D_MODEL), 1.0)
    norm_w = 1.0 + nrm(ks[1], (DEPTH, D_MODEL), 0.01)
    w_in = nrm(ks[2], (DEPTH, D_MODEL, IN_W), D_MODEL ** -0.5)
    conv_qkv_w = nrm(ks[3], (DEPTH, SHORT_CONV, 2 * GDN_QK + GDN_W), SHORT_CONV ** -0.5)
    a_log = jnp.log(jax.random.uniform(ks[4], (DEPTH, GDN_HEADS), f32, 1.0, 16.0))
    dt = jnp.exp(jax.random.uniform(ks[5], (DEPTH, GDN_HEADS), f32, math.log(1e-3), math.log(1e-1)))
    dt_bias = dt + jnp.log(-jnp.expm1(-dt))
    gdn_norm_w = 1.0 + nrm(ks[6], (DEPTH, GDN_DV), 0.01)
    conf_dw_w = nrm(ks[7], (DEPTH, CONV_WIDTH, CONV_CH), CONV_WIDTH ** -0.5)
    conf_dw_b = nrm(ks[8], (DEPTH, CONV_CH), 0.01)
    conf_ln_w = 1.0 + nrm(ks[9], (DEPTH, CONV_CH), 0.01)
    conf_ln_b = nrm(ks[10], (DEPTH, CONV_CH), 0.01)
    conf_pw_w = nrm(ks[11], (DEPTH, CONV_CH, CONV_CH), CONV_CH ** -0.5)
    w_out = nrm(ks[12], (DEPTH, MIX_W, D_MODEL), MIX_W ** -0.5)
    final_norm_w = 1.0 + nrm(ks[13], (D_MODEL,), 0.01)
    return {"x": x, "norm_w": norm_w, "w_in": w_in, "conv_qkv_w": conv_qkv_w, "a_log": a_log,
            "dt_bias": dt_bias, "gdn_norm_w": gdn_norm_w, "conf_dw_w": conf_dw_w,
            "conf_dw_b": conf_dw_b, "conf_ln_w": conf_ln_w, "conf_ln_b": conf_ln_b,
            "conf_pw_w": conf_pw_w, "w_out": w_out, "final_norm_w": final_norm_w}


def reference(x, norm_w, w_in, conv_qkv_w, a_log, dt_bias, gdn_norm_w, conf_dw_w, conf_dw_b,
              conf_ln_w, conf_ln_b, conf_pw_w, w_out, final_norm_w):
    B, S, _ = x.shape
    cos, sin = rope_tables(S)
    cuts, acc = [], 0
    for n in IN_SPLITS[:-1]:
        acc += n
        cuts.append(acc)
    for l in range(DEPTH):
        h = rms_norm(x, norm_w[l])
        u = h @ w_in[l].astype(h.dtype)
        (c_in, c_gate, g_q, g_k, g_v, g_z, g_b, g_a,
         a_q, a_k, a_v, a_gate) = jnp.split(u, cuts, axis=-1)
        y_conv = conformer_conv(c_in, conf_dw_w[l], conf_dw_b[l], conf_ln_w[l], conf_ln_b[l],
                                conf_pw_w[l]) * jax.nn.silu(c_gate)
        y_gdn = gated_deltanet(g_q, g_k, g_v, g_z, g_b, g_a, conv_qkv_w[l], a_log[l], dt_bias[l],
                               gdn_norm_w[l])
        y_att = dilated_attention(a_q, a_k, a_v, cos, sin) * jax.nn.silu(a_gate)
        y = jnp.concatenate([y_conv, y_gdn, y_att], axis=-1)
        x = x + y @ w_out[l].astype(y.dtype)
    return rms_norm(x, final_norm_w)
```

```python
import functools
import math

import jax
import jax.numpy as jnp
from jax import lax
from jax.experimental import pallas as pl
from jax.experimental.pallas import tpu as pltpu

F32 = jnp.float32
BF16 = jnp.bfloat16

D_MODEL = 2048
GDN_DK = 128
GDN_DV = 128
GDN_W = 768
GDN_HEADS = 6
ATT_HD = 64
ATT_W = 768
ATT_HEADS = 12
CONV_CH = 512
CONV_WIDTH = 31
SHORT_CONV = 4
GDN_CHUNK = 64
ROPE_THETA = 500000.0
ROPE_DIM = 16
DIL_PATTERNS = ((128, 1), (512, 4), (2048, 16))
ATT_BLOCK = 128
NEG_INF = -1e30
MAIN_W = 2 * CONV_CH + CONV_CH + 4 * GDN_W + 4 * ATT_W
BA_W = 256

COL_GQ, COL_GK, COL_GV, COL_GZ, COL_AQ, COL_AK, COL_AV, COL_AG = 2, 3, 4, 5, 6, 7, 8, 9
N_MAIN_BLOCKS = MAIN_W // 768

LANES = 128
SUBLANES = 8
VMEM_LIMIT = 56 * 1024 * 1024
INPROJ_TM, INPROJ_TN = 1024, 768
OUTPROJ_TM = 512
CONV_TS, CONV_RC, CONV_HALO = 256, 32, 32
GDN_TS = 512
PREP_TS = 512
COMB_TS = 512


def _cparams(sem):
    return pltpu.CompilerParams(dimension_semantics=sem, vmem_limit_bytes=VMEM_LIMIT)


def _sigmoid(x):
    return 1.0 / (1.0 + jnp.exp(-x))


def _silu(x):
    return x * _sigmoid(x)


def _dot(a, b):
    return jnp.dot(a, b, preferred_element_type=F32)


def _dot_nt(a, b):
    return lax.dot_general(a, b, (((1,), (1,)), ((), ())), preferred_element_type=F32)


def _inproj_kernel(x_ref, nw_ref, w_ref, wba_ref, u_ref, ba_ref, xn_ref):
    @pl.when(pl.program_id(1) == 0)
    def _():
        x = x_ref[...]
        ms = jnp.mean(x * x, axis=-1, keepdims=True)
        xn = ((x * lax.rsqrt(ms + 1e-6)) * nw_ref[...]).astype(BF16)
        xn_ref[...] = xn
        ba_ref[...] = _dot(xn, wba_ref[...])

    u_ref[...] = _dot(xn_ref[...], w_ref[...])


def _in_proj(x2, norm_w, w_main, w_ba):
    m = x2.shape[0]
    tm, tn = INPROJ_TM, INPROJ_TN
    return pl.pallas_call(
        _inproj_kernel,
        grid=(m // tm, MAIN_W // tn),
        in_specs=[
            pl.BlockSpec((tm, D_MODEL), lambda i, j: (i, 0)),
            pl.BlockSpec((1, D_MODEL), lambda i, j: (0, 0)),
            pl.BlockSpec((D_MODEL, tn), lambda i, j: (0, j)),
            pl.BlockSpec((D_MODEL, BA_W), lambda i, j: (0, 0)),
        ],
        out_specs=[
            pl.BlockSpec((tm, tn), lambda i, j: (i, j)),
            pl.BlockSpec((tm, BA_W), lambda i, j: (i, 0)),
        ],
        out_shape=[jax.ShapeDtypeStruct((m, MAIN_W), F32), jax.ShapeDtypeStruct((m, BA_W), F32)],
        scratch_shapes=[pltpu.VMEM((tm, D_MODEL), BF16)],
        compiler_params=_cparams(("arbitrary", "arbitrary")),
        name="in_proj",
    )(x2, norm_w, w_main, w_ba)


def _outproj_kernel(yc_ref, yg_ref, ya_ref, x_ref, wc_ref, wg_ref, wa_ref, fw_ref, o_ref, *, final):
    acc = _dot(yc_ref[...], wc_ref[...]) + _dot(yg_ref[...], wg_ref[...]) + _dot(ya_ref[...], wa_ref[...])
    xo = x_ref[...] + acc
    if final:
        ms = jnp.mean(xo * xo, axis=-1, keepdims=True)
        xo = (xo * lax.rsqrt(ms + 1e-6)) * fw_ref[...]
    o_ref[...] = xo


def _out_proj(y_conv, y_gdn, y_att, x2, w_c, w_g, w_a, final_w, final):
    m = x2.shape[0]
    tm = OUTPROJ_TM
    row = lambda i: (i, 0)
    fixed = lambda i: (0, 0)
    return pl.pallas_call(
        functools.partial(_outproj_kernel, final=final),
        grid=(m // tm,),
        in_specs=[
            pl.BlockSpec((tm, CONV_CH), row),
            pl.BlockSpec((tm, GDN_W), row),
            pl.BlockSpec((tm, ATT_W), row),
            pl.BlockSpec((tm, D_MODEL), row),
            pl.BlockSpec((CONV_CH, D_MODEL), fixed),
            pl.BlockSpec((GDN_W, D_MODEL), fixed),
            pl.BlockSpec((ATT_W, D_MODEL), fixed),
            pl.BlockSpec((1, D_MODEL), fixed),
        ],
        out_specs=pl.BlockSpec((tm, D_MODEL), row),
        out_shape=jax.ShapeDtypeStruct((m, D_MODEL), F32),
        compiler_params=_cparams(("arbitrary",)),
        name="out_proj_final" if final else "out_proj",
    )(y_conv, y_gdn, y_att, x2, w_c, w_g, w_a, final_w)


def _conv_kernel(cin_ref, cgate_ref, dww_ref, dwb_ref, lnw_ref, lnb_ref, pw_ref, y_ref, hbuf, act):
    ts, rc, halo = CONV_TS, CONV_RC, CONV_HALO
    s = pl.program_id(1)

    @pl.when(s == 0)
    def _():
        hbuf[0:halo, :] = jnp.zeros((halo, CONV_CH), F32)

    @pl.when(s > 0)
    def _():
        hbuf[0:halo, :] = hbuf[ts:ts + halo, :]

    u = cin_ref[0]
    hbuf[halo:halo + ts, :] = u[:, :CONV_CH] * _sigmoid(u[:, CONV_CH:])

    base = halo - (CONV_WIDTH - 1)
    for c in range(ts // rc):
        r0 = c * rc
        acc = jnp.zeros((rc, CONV_CH), F32) + dwb_ref[...]
        for k in range(CONV_WIDTH):
            acc = acc + dww_ref[k:k + 1, :] * hbuf[r0 + base + k:r0 + base + k + rc, :]
        mu = jnp.mean(acc, axis=-1, keepdims=True)
        d = acc - mu
        var = jnp.mean(d * d, axis=-1, keepdims=True)
        hn = d * lax.rsqrt(var + 1e-5) * lnw_ref[...] + lnb_ref[...]
        act[r0:r0 + rc, :] = _silu(hn).astype(BF16)

    y = _dot(act[...], pw_ref[...]) * _silu(cgate_ref[0])
    y_ref[0] = y.astype(BF16)


def _conv_branch(u3, dw_w, dw_b, ln_w, ln_b, pw_w):
    b, s, _ = u3.shape
    ts = CONV_TS
    fixed = lambda bi, si: (0, 0)
    return pl.pallas_call(
        _conv_kernel,
        grid=(b, s // ts),
        in_specs=[
            pl.BlockSpec((1, ts, 2 * CONV_CH), lambda bi, si: (bi, si, 0)),
            pl.BlockSpec((1, ts, CONV_CH), lambda bi, si: (bi, si, 2)),
            pl.BlockSpec((32, CONV_CH), fixed),
            pl.BlockSpec((1, CONV_CH), fixed),
            pl.BlockSpec((1, CONV_CH), fixed),
            pl.BlockSpec((1, CONV_CH), fixed),
            pl.BlockSpec((CONV_CH, CONV_CH), fixed),
        ],
        out_specs=pl.BlockSpec((1, ts, CONV_CH), lambda bi, si: (bi, si, 0)),
        out_shape=jax.ShapeDtypeStruct((b, s, CONV_CH), BF16),
        scratch_shapes=[pltpu.VMEM((ts + CONV_HALO, CONV_CH), F32), pltpu.VMEM((ts, CONV_CH), BF16)],
        compiler_params=_cparams(("arbitrary", "arbitrary")),
        name="conv_branch",
    )(u3, u3, dw_w, dw_b, ln_w, ln_b, pw_w)


def _split3(x):
    hi = x.astype(BF16)
    r1 = x - hi.astype(F32)
    mid = r1.astype(BF16)
    lo = (r1 - mid.astype(F32)).astype(BF16)
    return hi, mid, lo


def _unit_lower_inverse(a, ii, jj):
    c = a.shape[0]
    same = lambda log2: jnp.right_shift(ii, log2) == jnp.right_shift(jj, log2)
    r = -jnp.where(same(1), a, 0.0)
    bs, lg = 2, 1
    while bs < c:
        e = jnp.where(same(lg + 1), jnp.where(same(lg), 0.0, a), 0.0)
        lg += 1
        rb = r.astype(BF16)
        x = _dot(rb, e.astype(BF16))
        y = _dot((e + x).astype(BF16), rb)
        r = r - e - x - y
        bs *= 2
    return r


def _gdn_kernel(q_ref, k_ref, v_ref, z_ref, ba_ref, cw_ref, gp_ref, nw_ref, y_ref,
                carry, xbuf, qn, kn, vn, gcum, beta, state):
    ts, c = GDN_TS, GDN_CHUNK
    w3 = 3 * GDN_W
    s = pl.program_id(1)

    @pl.when(s == 0)
    def _():
        carry[...] = jnp.zeros(carry.shape, F32)
        state[...] = jnp.zeros(state.shape, F32)

    xbuf[0:SUBLANES, :] = carry[...]
    xbuf[SUBLANES:SUBLANES + ts, 0:GDN_W] = q_ref[0]
    xbuf[SUBLANES:SUBLANES + ts, GDN_W:2 * GDN_W] = k_ref[0]
    xbuf[SUBLANES:SUBLANES + ts, 2 * GDN_W:w3] = v_ref[0]
    carry[...] = xbuf[ts:ts + SUBLANES, :]
    base = SUBLANES - (SHORT_CONV - 1)
    for part, dst in ((0, qn), (1, kn), (2, vn)):
        for h in range(GDN_HEADS):
            lo = part * GDN_W + h * GDN_DK
            acc = jnp.zeros((ts, GDN_DK), F32)
            for t in range(SHORT_CONV):
                acc = acc + cw_ref[t:t + 1, lo:lo + GDN_DK] * xbuf[base + t:base + t + ts, lo:lo + GDN_DK]
            acc = _silu(acc)
            if part == 0:
                acc = (acc * lax.rsqrt(jnp.sum(acc * acc, axis=-1, keepdims=True) + 1e-6)) * (GDN_DK ** -0.5)
            elif part == 1:
                acc = acc * lax.rsqrt(jnp.sum(acc * acc, axis=-1, keepdims=True) + 1e-6)
            dst[:, h * GDN_DK:(h + 1) * GDN_DK] = acc

    ba = ba_ref[0]
    beta[...] = _sigmoid(ba[:, 0:LANES])
    xa = ba[:, LANES:2 * LANES] + gp_ref[1:2, :]
    softplus = jnp.maximum(xa, 0.0) + jnp.log(1.0 + jnp.exp(-jnp.abs(xa)))
    gstep = -jnp.exp(gp_ref[0:1, :]) * softplus
    ri = lax.broadcasted_iota(jnp.int32, (ts, ts), 0)
    ci = lax.broadcasted_iota(jnp.int32, (ts, ts), 1)
    lg_c = c.bit_length() - 1
    same_chunk = jnp.right_shift(ri, lg_c) == jnp.right_shift(ci, lg_c)
    tril = jnp.where(same_chunk, jnp.where(ri >= ci, 1.0, 0.0), 0.0).astype(BF16)
    g_hi, g_mid, g_lo = _split3(gstep)
    gcum[...] = _dot(tril, g_hi) + _dot(tril, g_mid) + _dot(tril, g_lo)

    ii = lax.broadcasted_iota(jnp.int32, (c, c), 0)
    jj = lax.broadcasted_iota(jnp.int32, (c, c), 1)
    causal = ii >= jj
    strict = ii > jj

    def chunk_body(ci_, _):
        r0 = pl.multiple_of(ci_ * c, c)
        g_blk = gcum[pl.ds(r0, c), :]
        g_t = g_blk.T
        b_blk = beta[pl.ds(r0, c), :]
        for h in range(GDN_HEADS):
            hs = slice(h * GDN_DK, (h + 1) * GDN_DK)
            gc = g_blk[:, h:h + 1]
            gr = g_t[h:h + 1, :]
            bc = b_blk[:, h:h + 1]
            qh = qn[pl.ds(r0, c), hs]
            kh = kn[pl.ds(r0, c), hs]
            vh = vn[pl.ds(r0, c), hs]
            kb = kh.astype(BF16)
            qb = qh.astype(BF16)
            decay = jnp.where(causal, jnp.exp(jnp.where(causal, gc - gr, 0.0)), 0.0)
            a = jnp.where(strict, bc * _dot_nt(kb, kb) * decay, 0.0)
            t_r = _unit_lower_inverse(a, ii, jj)
            t_b = (t_r + jnp.where(ii == jj, 1.0, 0.0)).astype(BF16)
            qk = jnp.where(causal, _dot_nt(qb, kb) * decay, 0.0)

            s_h = state[h]
            s_b = s_h.astype(BF16)
            eg = jnp.exp(gc)
            u = bc * (vh - eg * _dot(kb, s_b))
            v_new = _dot(t_b, u.astype(BF16))
            vb = v_new.astype(BF16)
            o = _dot((qh * eg).astype(BF16), s_b) + _dot(qk.astype(BF16), vb)
            g_last = gc[c - 1:c, :]
            k_dec = kh * jnp.exp(g_last - gc)
            state[h] = s_h * jnp.exp(g_last) + _dot(k_dec.T.astype(BF16), vb)

            on = (o * lax.rsqrt(jnp.mean(o * o, axis=-1, keepdims=True) + 1e-6)) * nw_ref[...]
            y_ref[0, pl.ds(r0, c), hs] = (on * _silu(z_ref[0, pl.ds(r0, c), hs])).astype(BF16)
        return 0

    lax.fori_loop(0, ts // c, chunk_body, 0)


def _gdn_branch(u3, ba3, conv_w, gparams, norm_w):
    b, s, _ = u3.shape
    ts = GDN_TS
    fixed = lambda bi, si: (0, 0)
    col = lambda j: (lambda bi, si: (bi, si, j))
    return pl.pallas_call(
        _gdn_kernel,
        grid=(b, s // ts),
        in_specs=[
            pl.BlockSpec((1, ts, GDN_W), col(COL_GQ)),
            pl.BlockSpec((1, ts, GDN_W), col(COL_GK)),
            pl.BlockSpec((1, ts, GDN_W), col(COL_GV)),
            pl.BlockSpec((1, ts, GDN_W), col(COL_GZ)),
            pl.BlockSpec((1, ts, BA_W), col(0)),
            pl.BlockSpec((SUBLANES, 3 * GDN_W), fixed),
            pl.BlockSpec((SUBLANES, LANES), fixed),
            pl.BlockSpec((1, GDN_DV), fixed),
        ],
        out_specs=pl.BlockSpec((1, ts, GDN_W), col(0)),
        out_shape=jax.ShapeDtypeStruct((b, s, GDN_W), BF16),
        scratch_shapes=[
            pltpu.VMEM((SUBLANES, 3 * GDN_W), F32),
            pltpu.VMEM((ts + SUBLANES, 3 * GDN_W), F32),
            pltpu.VMEM((ts, GDN_W), F32),
            pltpu.VMEM((ts, GDN_W), F32),
            pltpu.VMEM((ts, GDN_W), F32),
            pltpu.VMEM((ts, LANES), F32),
            pltpu.VMEM((ts, LANES), F32),
            pltpu.VMEM((GDN_HEADS, GDN_DK, GDN_DV), F32),
        ],
        compiler_params=_cparams(("arbitrary", "arbitrary")),
        name="gdn_branch",
    )(u3, u3, u3, u3, ba3, conv_w, gparams, norm_w)


def _attprep_kernel(q_ref, k_ref, v_ref, c_ref, s1_ref, s2_ref, qo_ref, ko_ref, vo_ref):
    cs, s1, s2 = c_ref[...], s1_ref[...], s2_ref[...]
    for p in range(ATT_W // LANES):
        sl = slice(p * LANES, (p + 1) * LANES)
        for src, dst, scale in ((q_ref, qo_ref, ATT_HD ** -0.5), (k_ref, ko_ref, None)):
            x = src[0, :, sl]
            r = x * cs + pltpu.roll(x, LANES - ROPE_DIM // 2, axis=1) * s1 + pltpu.roll(x, ROPE_DIM // 2, axis=1) * s2
            if scale is not None:
                r = r * scale
            dst[0, :, sl] = r.astype(BF16)
    vo_ref[0] = v_ref[0].astype(BF16)


def _att_prep(u3, cos_t, sin_a, sin_b):
    b, s, _ = u3.shape
    ts = PREP_TS
    col = lambda j: (lambda bi, si: (bi, si, j))
    tab = pl.BlockSpec((ts, LANES), lambda bi, si: (si, 0))
    out = jax.ShapeDtypeStruct((b, s, ATT_W), BF16)
    return pl.pallas_call(
        _attprep_kernel,
        grid=(b, s // ts),
        in_specs=[pl.BlockSpec((1, ts, ATT_W), col(COL_AQ)), pl.BlockSpec((1, ts, ATT_W), col(COL_AK)),
                  pl.BlockSpec((1, ts, ATT_W), col(COL_AV)), tab, tab, tab],
        out_specs=[pl.BlockSpec((1, ts, ATT_W), col(0))] * 3,
        out_shape=[out, out, out],
        compiler_params=_cparams(("arbitrary", "arbitrary")),
        name="att_prep",
    )(u3, u3, u3, cos_t, sin_a, sin_b)


def _att_kernel(q_ref, k_ref, v_ref, o_ref, lse_ref, kprev, vprev):
    blk = ATT_BLOCK
    n = pl.program_id(2)

    @pl.when(n == 0)
    def _():
        kprev[...] = jnp.zeros(kprev.shape, BF16)
        vprev[...] = jnp.zeros(vprev.shape, BF16)

    row = lax.broadcasted_iota(jnp.int32, (blk, 2 * blk), 0)
    col = lax.broadcasted_iota(jnp.int32, (blk, 2 * blk), 1)
    first = jnp.where(n > 0, 0, 4 * blk)
    ahead = col - row
    valid = jnp.where(col < blk, ahead - first, blk - ahead) >= 0
    lane = lax.broadcasted_iota(jnp.int32, (blk, LANES), 1)
    low = lane < ATT_HD

    lse_tile = jnp.zeros((blk, LANES), F32)
    for p in range(ATT_W // LANES):
        sl = slice(p * LANES, (p + 1) * LANES)
        kcat = jnp.concatenate([kprev[:, sl], k_ref[0, :, sl]], axis=0)
        vcat = jnp.concatenate([vprev[:, sl], v_ref[0, :, sl]], axis=0)
        qp = q_ref[0, :, sl]
        o_pair = None
        for hh in range(2):
            hm = low if hh == 0 else jnp.logical_not(low)
            qh = jnp.where(hm, qp, jnp.zeros_like(qp))
            sc = jnp.where(valid, _dot_nt(qh, kcat), NEG_INF)
            m = jnp.max(sc, axis=-1, keepdims=True)
            pe = jnp.exp(sc - m)
            l = jnp.sum(pe, axis=-1, keepdims=True)
            o = _dot(pe.astype(BF16), vcat) * (1.0 / l)
            o_pair = o if hh == 0 else jnp.where(low, o_pair, o)
            lse_tile = jnp.where(lane == 2 * p + hh, m + jnp.log(l), lse_tile)
        o_ref[0, :, sl] = o_pair
    lse_ref[0] = lse_tile
    kprev[...] = k_ref[0]
    vprev[...] = v_ref[0]


def _att_pattern(qb, kb, vb, dil):
    b, s, _ = qb.shape
    blk = ATT_BLOCK
    lr = s // dil
    view = lambda t: t.reshape(b, lr, dil * ATT_W)
    spec = pl.BlockSpec((1, blk, ATT_W), lambda bi, r, n: (bi, n, r))
    o, lse = pl.pallas_call(
        _att_kernel,
        grid=(b, dil, lr // blk),
        in_specs=[spec, spec, spec],
        out_specs=[spec, pl.BlockSpec((1, blk, LANES), lambda bi, r, n: (bi, n, r))],
        out_shape=[jax.ShapeDtypeStruct((b, lr, dil * ATT_W), F32),
                   jax.ShapeDtypeStruct((b, lr, dil * LANES), F32)],
        scratch_shapes=[pltpu.VMEM((blk, ATT_W), BF16), pltpu.VMEM((blk, ATT_W), BF16)],
        compiler_params=_cparams(("arbitrary", "arbitrary", "arbitrary")),
        name=f"att_dil{dil}",
    )(view(qb), view(kb), view(vb))
    return o.reshape(b, s, ATT_W), lse.reshape(b, s, LANES)


def _combine_kernel(o1_ref, o2_ref, o3_ref, l1_ref, l2_ref, l3_ref, g_ref, y_ref):
    l1, l2, l3 = l1_ref[0], l2_ref[0], l3_ref[0]
    mx = jnp.maximum(jnp.maximum(l1, l2), l3)
    e1, e2, e3 = jnp.exp(l1 - mx), jnp.exp(l2 - mx), jnp.exp(l3 - mx)
    inv = 1.0 / (e1 + e2 + e3)
    ws = (e1 * inv, e2 * inv, e3 * inv)
    low = lax.broadcasted_iota(jnp.int32, (COMB_TS, LANES), 1) < ATT_HD
    for p in range(ATT_W // LANES):
        sl = slice(p * LANES, (p + 1) * LANES)
        acc = None
        for w, o_ref in zip(ws, (o1_ref, o2_ref, o3_ref)):
            wp = jnp.where(low, w[:, 2 * p:2 * p + 1], w[:, 2 * p + 1:2 * p + 2])
            t = wp * o_ref[0, :, sl]
            acc = t if acc is None else acc + t
        y_ref[0, :, sl] = (acc * _silu(g_ref[0, :, sl])).astype(BF16)


def _att_combine(outs, lses, u3):
    b, s, _ = u3.shape
    ts = COMB_TS
    col = lambda j: (lambda bi, si: (bi, si, j))
    ospec = pl.BlockSpec((1, ts, ATT_W), col(0))
    lspec = pl.BlockSpec((1, ts, LANES), col(0))
    return pl.pallas_call(
        _combine_kernel,
        grid=(b, s // ts),
        in_specs=[ospec, ospec, ospec, lspec, lspec, lspec, pl.BlockSpec((1, ts, ATT_W), col(COL_AG))],
        out_specs=ospec,
        out_shape=jax.ShapeDtypeStruct((b, s, ATT_W), BF16),
        compiler_params=_cparams(("arbitrary", "arbitrary")),
        name="att_combine",
    )(*outs, *lses, u3)


def _rope_lane_tables(s):
    half = ROPE_DIM // 2
    inv = ROPE_THETA ** (-jnp.arange(half, dtype=F32) / half)
    ang = jnp.arange(s, dtype=F32)[:, None] * inv[None, :]
    cos, sin = jnp.cos(ang), jnp.sin(ang)
    ones = jnp.ones((s, ATT_HD - ROPE_DIM), F32)
    zeros = jnp.zeros((s, ATT_HD - half), F32)
    cos_h = jnp.concatenate([cos, cos, ones], axis=-1)
    sin_a = jnp.concatenate([-sin, zeros], axis=-1)
    sin_b = jnp.concatenate([jnp.zeros((s, half), F32), sin, zeros[:, half:]], axis=-1)
    rep = lambda t: jnp.concatenate([t, t], axis=-1)
    return rep(cos_h), rep(sin_a), rep(sin_b)


def kernel(x, norm_w, w_in, conv_qkv_w, a_log, dt_bias, gdn_norm_w, conf_dw_w, conf_dw_b, conf_ln_w,
           conf_ln_b, conf_pw_w, w_out, final_norm_w):
    b, s, d = x.shape
    depth = w_in.shape[0]
    m = b * s
    assert d == D_MODEL and m % INPROJ_TM == 0 and s % (DIL_PATTERNS[-1][1] * ATT_BLOCK) == 0

    ba0 = 2 * CONV_CH + CONV_CH + 4 * GDN_W
    w_main = jnp.concatenate([w_in[:, :, :ba0], w_in[:, :, ba0 + 2 * GDN_HEADS:]], axis=-1).astype(BF16)
    w_ba = jnp.zeros((depth, d, BA_W), F32)
    w_ba = w_ba.at[:, :, 0:GDN_HEADS].set(w_in[:, :, ba0:ba0 + GDN_HEADS])
    w_ba = w_ba.at[:, :, LANES:LANES + GDN_HEADS].set(w_in[:, :, ba0 + GDN_HEADS:ba0 + 2 * GDN_HEADS])
    w_ba = w_ba.astype(BF16)
    w_out_b = w_out.astype(BF16)
    pw_b = conf_pw_w.astype(BF16)
    dw_w = jnp.pad(conf_dw_w, ((0, 0), (0, 32 - CONV_WIDTH), (0, 0)))
    cw = jnp.pad(conv_qkv_w, ((0, 0), (0, SUBLANES - SHORT_CONV), (0, 0)))
    gparams = jnp.zeros((depth, SUBLANES, LANES), F32)
    gparams = gparams.at[:, 0, 0:GDN_HEADS].set(a_log).at[:, 1, 0:GDN_HEADS].set(dt_bias)
    cos_t, sin_a, sin_b = _rope_lane_tables(s)

    x2 = x.reshape(m, d)
    for l in range(depth):
        u, ba = _in_proj(x2, norm_w[l][None, :], w_main[l], w_ba[l])
        u3 = u.reshape(b, s, MAIN_W)
        y_conv = _conv_branch(u3, dw_w[l], conf_dw_b[l][None, :], conf_ln_w[l][None, :], conf_ln_b[l][None, :],
                              pw_b[l])
        y_gdn = _gdn_branch(u3, ba.reshape(b, s, BA_W), cw[l], gparams[l], gdn_norm_w[l][None, :])
        qb, kb, vb = _att_prep(u3, cos_t, sin_a, sin_b)
        outs, lses = [], []
        for _, dil in DIL_PATTERNS:
            o_g, lse_g = _att_pattern(qb, kb, vb, dil)
            outs.append(o_g)
            lses.append(lse_g)
        y_att = _att_combine(outs, lses, u3)
        x2 = _out_proj(y_conv.reshape(m, CONV_CH), y_gdn.reshape(m, GDN_W), y_att.reshape(m, ATT_W), x2,
                       w_out_b[l, :CONV_CH], w_out_b[l, CONV_CH:CONV_CH + GDN_W], w_out_b[l, CONV_CH + GDN_W:],
                       final_norm_w[None, :], final=(l == depth - 1))
    return x2.reshape(b, s, d)
```

```python
import functools

import jax
import jax.numpy as jnp
from jax import lax
from jax.experimental import pallas as pl
from jax.experimental.pallas import tpu as pltpu

F32 = jnp.float32
BF16 = jnp.bfloat16

D_MODEL = 2048
GDN_DK = 128
GDN_DV = 128
GDN_W = 768
GDN_HEADS = 6
ATT_HD = 64
ATT_W = 768
ATT_HEADS = 12
CONV_CH = 512
CONV_WIDTH = 31
SHORT_CONV = 4
GDN_CHUNK = 64
ROPE_THETA = 500000.0
ROPE_DIM = 16
DIL_PATTERNS = ((128, 1), (512, 4), (2048, 16))
ATT_BLOCK = 128
NEG_INF = -1e30
MAIN_W = 2 * CONV_CH + CONV_CH + 4 * GDN_W + 4 * ATT_W
BA_W = 256

COL_GQ, COL_GK, COL_GV, COL_GZ, COL_AQ, COL_AK, COL_AV, COL_AG = 2, 3, 4, 5, 6, 7, 8, 9

LANES = 128
SUBLANES = 8
VMEM_LIMIT = 60 * 1024 * 1024
INPROJ_TM, INPROJ_TN = 2048, 768
OUTPROJ_TM = 512
CONV_TS, CONV_RC, CONV_HALO = 256, 32, 32
GDN_TS = 512
GDN_TGROUP = 2
PREP_TS = 512
COMB_TS = 512


def _cparams(sem):
    return pltpu.CompilerParams(dimension_semantics=sem, vmem_limit_bytes=VMEM_LIMIT)


def _sigmoid(x):
    return 1.0 / (1.0 + jnp.exp(-x))


def _silu(x):
    return x * _sigmoid(x)


def _dot(a, b):
    return jnp.dot(a, b, preferred_element_type=F32)


def _dot_nt(a, b):
    return lax.dot_general(a, b, (((1,), (1,)), ((), ())), preferred_element_type=F32)


def _rms_norm(x, w):
    ms = jnp.mean(x * x, axis=-1, keepdims=True)
    return (x * lax.rsqrt(ms + 1e-6)) * w


def _prenorm_kernel(x_ref, nw_ref, xn_ref):
    xn_ref[...] = _rms_norm(x_ref[...], nw_ref[...]).astype(BF16)


def _pre_norm(x2, norm_w):
    m = x2.shape[0]
    tm = OUTPROJ_TM
    return pl.pallas_call(
        _prenorm_kernel,
        grid=(m // tm,),
        in_specs=[pl.BlockSpec((tm, D_MODEL), lambda i: (i, 0)), pl.BlockSpec((1, D_MODEL), lambda i: (0, 0))],
        out_specs=pl.BlockSpec((tm, D_MODEL), lambda i: (i, 0)),
        out_shape=jax.ShapeDtypeStruct((m, D_MODEL), BF16),
        compiler_params=_cparams(("arbitrary",)),
        name="pre_norm",
    )(x2, norm_w)


def _inproj_kernel(xn_ref, w_ref, wba_ref, u_ref, ba_ref):
    @pl.when(pl.program_id(1) == 0)
    def _():
        ba_ref[...] = _dot(xn_ref[...], wba_ref[...])

    u_ref[...] = _dot(xn_ref[...], w_ref[...])


def _in_proj(xn, w_main, w_ba):
    m = xn.shape[0]
    tm, tn = INPROJ_TM, INPROJ_TN
    return pl.pallas_call(
        _inproj_kernel,
        grid=(m // tm, MAIN_W // tn),
        in_specs=[
            pl.BlockSpec((tm, D_MODEL), lambda i, j: (i, 0)),
            pl.BlockSpec((D_MODEL, tn), lambda i, j: (0, j)),
            pl.BlockSpec((D_MODEL, BA_W), lambda i, j: (0, 0)),
        ],
        out_specs=[
            pl.BlockSpec((tm, tn), lambda i, j: (i, j)),
            pl.BlockSpec((tm, BA_W), lambda i, j: (i, 0)),
        ],
        out_shape=[jax.ShapeDtypeStruct((m, MAIN_W), F32), jax.ShapeDtypeStruct((m, BA_W), F32)],
        compiler_params=_cparams(("arbitrary", "arbitrary")),
        name="in_proj",
    )(xn, w_main, w_ba)


def _outproj_kernel(yc_ref, yg_ref, ya_ref, x_ref, wc_ref, wg_ref, wa_ref, nw_ref, o_ref, *maybe_xn_ref, final):
    acc = _dot(yc_ref[...], wc_ref[...]) + _dot(yg_ref[...], wg_ref[...]) + _dot(ya_ref[...], wa_ref[...])
    xo = x_ref[...] + acc
    if final:
        o_ref[...] = _rms_norm(xo, nw_ref[...])
    else:
        o_ref[...] = xo
        maybe_xn_ref[0][...] = _rms_norm(xo, nw_ref[...]).astype(BF16)


def _out_proj(y_conv, y_gdn, y_att, x2, w_c, w_g, w_a, next_norm_w, final):
    m = x2.shape[0]
    tm = OUTPROJ_TM
    row = lambda i: (i, 0)
    fixed = lambda i: (0, 0)
    out_specs = [pl.BlockSpec((tm, D_MODEL), row)]
    out_shape = [jax.ShapeDtypeStruct((m, D_MODEL), F32)]
    if not final:
        out_specs.append(pl.BlockSpec((tm, D_MODEL), row))
        out_shape.append(jax.ShapeDtypeStruct((m, D_MODEL), BF16))
    return pl.pallas_call(
        functools.partial(_outproj_kernel, final=final),
        grid=(m // tm,),
        in_specs=[
            pl.BlockSpec((tm, CONV_CH), row),
            pl.BlockSpec((tm, GDN_W), row),
            pl.BlockSpec((tm, ATT_W), row),
            pl.BlockSpec((tm, D_MODEL), row),
            pl.BlockSpec((CONV_CH, D_MODEL), fixed),
            pl.BlockSpec((GDN_W, D_MODEL), fixed),
            pl.BlockSpec((ATT_W, D_MODEL), fixed),
            pl.BlockSpec((1, D_MODEL), fixed),
        ],
        out_specs=out_specs,
        out_shape=out_shape,
        compiler_params=_cparams(("arbitrary",)),
        name="out_proj_final" if final else "out_proj",
    )(y_conv, y_gdn, y_att, x2, w_c, w_g, w_a, next_norm_w)


def _conv_kernel(cin_ref, cgate_ref, dww_ref, dwb_ref, lnw_ref, lnb_ref, pw_ref, y_ref, hbuf, hsh, act):
    ts, rc, halo = CONV_TS, CONV_RC, CONV_HALO
    s = pl.program_id(1)

    @pl.when(s == 0)
    def _():
        hbuf[0:halo, :] = jnp.zeros((halo, CONV_CH), F32)

    @pl.when(s > 0)
    def _():
        hbuf[0:halo, :] = hbuf[ts:ts + halo, :]

    u = cin_ref[0]
    hbuf[halo:halo + ts, :] = u[:, :CONV_CH] * _sigmoid(u[:, CONV_CH:])

    span = ts + halo - SUBLANES
    for r in range(1, SUBLANES):
        hsh[r - 1, 0:span, :] = hbuf[r:r + span, :]

    base = halo - (CONV_WIDTH - 1)
    for c in range(ts // rc):
        r0 = c * rc
        acc = jnp.zeros((rc, CONV_CH), F32) + dwb_ref[...]
        for k in range(CONV_WIDTH):
            shift = (base + k) % SUBLANES
            lo = r0 + base + k - shift
            rows = hbuf[lo:lo + rc, :] if shift == 0 else hsh[shift - 1, lo:lo + rc, :]
            acc = acc + dww_ref[k:k + 1, :] * rows
        mu = jnp.mean(acc, axis=-1, keepdims=True)
        d = acc - mu
        var = jnp.mean(d * d, axis=-1, keepdims=True)
        hn = d * lax.rsqrt(var + 1e-5) * lnw_ref[...] + lnb_ref[...]
        act[r0:r0 + rc, :] = _silu(hn).astype(BF16)

    y = _dot(act[...], pw_ref[...]) * _silu(cgate_ref[0])
    y_ref[0] = y.astype(BF16)


def _conv_branch(u3, dw_w, dw_b, ln_w, ln_b, pw_w):
    b, s, _ = u3.shape
    ts = CONV_TS
    fixed = lambda bi, si: (0, 0)
    return pl.pallas_call(
        _conv_kernel,
        grid=(b, s // ts),
        in_specs=[
            pl.BlockSpec((1, ts, 2 * CONV_CH), lambda bi, si: (bi, si, 0)),
            pl.BlockSpec((1, ts, CONV_CH), lambda bi, si: (bi, si, 2)),
            pl.BlockSpec((32, CONV_CH), fixed),
            pl.BlockSpec((1, CONV_CH), fixed),
            pl.BlockSpec((1, CONV_CH), fixed),
            pl.BlockSpec((1, CONV_CH), fixed),
            pl.BlockSpec((CONV_CH, CONV_CH), fixed),
        ],
        out_specs=pl.BlockSpec((1, ts, CONV_CH), lambda bi, si: (bi, si, 0)),
        out_shape=jax.ShapeDtypeStruct((b, s, CONV_CH), BF16),
        scratch_shapes=[pltpu.VMEM((ts + CONV_HALO, CONV_CH), F32),
                        pltpu.VMEM((SUBLANES - 1, ts + CONV_HALO, CONV_CH), F32),
                        pltpu.VMEM((ts, CONV_CH), BF16)],
        compiler_params=_cparams(("arbitrary", "arbitrary")),
        name="conv_branch",
    )(u3, u3, dw_w, dw_b, ln_w, ln_b, pw_w)


def _split3(x):
    hi = x.astype(BF16)
    r1 = x - hi.astype(F32)
    mid = r1.astype(BF16)
    lo = (r1 - mid.astype(F32)).astype(BF16)
    return hi, mid, lo


def _unit_lower_inverse_many(a_list, ii, jj):
    c = a_list[0].shape[0]
    same = lambda log2: jnp.right_shift(ii, log2) == jnp.right_shift(jj, log2)
    in_blk = same(1)
    rs = [-jnp.where(in_blk, a, 0.0) for a in a_list]
    lg = 1
    while (1 << lg) < c:
        in_pair = same(lg + 1)
        es = [jnp.where(in_pair, jnp.where(in_blk, 0.0, a), 0.0) for a in a_list]
        rbs = [r.astype(BF16) for r in rs]
        xs = [_dot(rb, e.astype(BF16)) for rb, e in zip(rbs, es)]
        ys = [_dot((e + x).astype(BF16), rb) for e, x, rb in zip(es, xs, rbs)]
        rs = [r - e - x - y for r, e, x, y in zip(rs, es, xs, ys)]
        in_blk = in_pair
        lg += 1
    return rs


def _gdn_kernel(q_ref, k_ref, v_ref, z_ref, ba_ref, cw_ref, gp_ref, nw_ref, tril_ref, y_ref,
                carry, xbuf, qn, kn, vn, gcum, beta, tbuf, qkbuf, state):
    ts, c = GDN_TS, GDN_CHUNK
    w3 = 3 * GDN_W
    heads = range(GDN_HEADS)
    hsl = lambda h: slice(h * GDN_DK, (h + 1) * GDN_DK)
    s = pl.program_id(1)

    @pl.when(s == 0)
    def _():
        carry[...] = jnp.zeros(carry.shape, F32)
        state[...] = jnp.zeros(state.shape, F32)

    xbuf[0:SUBLANES, :] = carry[...]
    xbuf[SUBLANES:SUBLANES + ts, 0:GDN_W] = q_ref[0]
    xbuf[SUBLANES:SUBLANES + ts, GDN_W:2 * GDN_W] = k_ref[0]
    xbuf[SUBLANES:SUBLANES + ts, 2 * GDN_W:w3] = v_ref[0]
    carry[...] = xbuf[ts:ts + SUBLANES, :]
    base = SUBLANES - (SHORT_CONV - 1)
    for part, dst in ((0, qn), (1, kn), (2, vn)):
        for h in heads:
            lo = part * GDN_W + h * GDN_DK
            acc = jnp.zeros((ts, GDN_DK), F32)
            for t in range(SHORT_CONV):
                acc = acc + cw_ref[t:t + 1, lo:lo + GDN_DK] * xbuf[base + t:base + t + ts, lo:lo + GDN_DK]
            acc = _silu(acc)
            if part == 0:
                acc = (acc * lax.rsqrt(jnp.sum(acc * acc, axis=-1, keepdims=True) + 1e-6)) * (GDN_DK ** -0.5)
            elif part == 1:
                acc = acc * lax.rsqrt(jnp.sum(acc * acc, axis=-1, keepdims=True) + 1e-6)
            dst[:, hsl(h)] = acc

    ba = ba_ref[0]
    beta[...] = _sigmoid(ba[:, 0:LANES])
    xa = ba[:, LANES:2 * LANES] + gp_ref[1:2, :]
    softplus = jnp.maximum(xa, 0.0) + jnp.log(1.0 + jnp.exp(-jnp.abs(xa)))
    gstep = -jnp.exp(gp_ref[0:1, :]) * softplus
    tril = tril_ref[...]
    g_hi, g_mid, g_lo = _split3(gstep)
    gcum[...] = _dot(tril, g_hi) + _dot(tril, g_mid) + _dot(tril, g_lo)

    ii = lax.broadcasted_iota(jnp.int32, (c, c), 0)
    jj = lax.broadcasted_iota(jnp.int32, (c, c), 1)
    causal = ii >= jj
    strict = ii > jj
    eye = jnp.where(ii == jj, 1.0, 0.0)

    def t_body(gi, _):
        items = []
        for cc in range(GDN_TGROUP):
            ci_ = gi * GDN_TGROUP + cc
            r0 = pl.multiple_of(ci_ * c, c)
            g_blk = gcum[pl.ds(r0, c), :]
            g_t = g_blk.T
            b_blk = beta[pl.ds(r0, c), :]
            for h in heads:
                items.append((ci_ * GDN_HEADS + h, r0, h, g_blk[:, h:h + 1], g_t[h:h + 1, :], b_blk[:, h:h + 1]))
        kbs = [kn[pl.ds(r0, c), hsl(h)].astype(BF16) for _, r0, h, _, _, _ in items]
        qbs = [qn[pl.ds(r0, c), hsl(h)].astype(BF16) for _, r0, h, _, _, _ in items]
        decays = [jnp.where(causal, jnp.exp(jnp.where(causal, gc - gr, 0.0)), 0.0) for _, _, _, gc, gr, _ in items]
        kks = [_dot_nt(kb, kb) for kb in kbs]
        qks = [_dot_nt(qb, kb) for qb, kb in zip(qbs, kbs)]
        a_s = [jnp.where(strict, it[5] * kk * dc, 0.0) for it, kk, dc in zip(items, kks, decays)]
        for it, qk, dc in zip(items, qks, decays):
            qkbuf[it[0]] = jnp.where(causal, qk * dc, 0.0).astype(BF16)
        rs = _unit_lower_inverse_many(a_s, ii, jj)
        for it, r in zip(items, rs):
            tbuf[it[0]] = (r + eye).astype(BF16)
        return 0

    lax.fori_loop(0, ts // (c * GDN_TGROUP), t_body, 0)

    def chunk_body(ci_, _):
        r0 = pl.multiple_of(ci_ * c, c)
        rows = pl.ds(r0, c)
        g_blk = gcum[rows, :]
        b_blk = beta[rows, :]
        gcs = [g_blk[:, h:h + 1] for h in heads]
        egs = [jnp.exp(gc) for gc in gcs]
        glasts = [gc[c - 1:c, :] for gc in gcs]
        khs = [kn[rows, hsl(h)] for h in heads]
        s_hs = [state[h] for h in heads]
        s_bs = [sh.astype(BF16) for sh in s_hs]
        kss = [_dot(kh.astype(BF16), sb) for kh, sb in zip(khs, s_bs)]
        us = [b_blk[:, h:h + 1] * (vn[rows, hsl(h)] - egs[h] * kss[h]) for h in heads]
        vbs = [_dot(tbuf[ci_ * GDN_HEADS + h], us[h].astype(BF16)).astype(BF16) for h in heads]
        for h in heads:
            k_dec = khs[h] * jnp.exp(glasts[h] - gcs[h])
            state[h] = s_hs[h] * jnp.exp(glasts[h]) + _dot(k_dec.T.astype(BF16), vbs[h])
        os_ = [_dot((qn[rows, hsl(h)] * egs[h]).astype(BF16), s_bs[h]) + _dot(qkbuf[ci_ * GDN_HEADS + h], vbs[h])
               for h in heads]
        for h in heads:
            o = os_[h]
            on = (o * lax.rsqrt(jnp.mean(o * o, axis=-1, keepdims=True) + 1e-6)) * nw_ref[...]
            y_ref[0, rows, hsl(h)] = (on * _silu(z_ref[0, rows, hsl(h)])).astype(BF16)
        return 0

    lax.fori_loop(0, ts // c, chunk_body, 0)


def _gdn_branch(u3, ba3, conv_w, gparams, norm_w):
    b, s, _ = u3.shape
    ts, c = GDN_TS, GDN_CHUNK
    fixed = lambda bi, si: (0, 0)
    col = lambda j: (lambda bi, si: (bi, si, j))
    ri = lax.broadcasted_iota(jnp.int32, (ts, ts), 0)
    ci = lax.broadcasted_iota(jnp.int32, (ts, ts), 1)
    tril = ((ri // c == ci // c) & (ri >= ci)).astype(BF16)
    n_mats = (ts // c) * GDN_HEADS
    return pl.pallas_call(
        _gdn_kernel,
        grid=(b, s // ts),
        in_specs=[
            pl.BlockSpec((1, ts, GDN_W), col(COL_GQ)),
            pl.BlockSpec((1, ts, GDN_W), col(COL_GK)),
            pl.BlockSpec((1, ts, GDN_W), col(COL_GV)),
            pl.BlockSpec((1, ts, GDN_W), col(COL_GZ)),
            pl.BlockSpec((1, ts, BA_W), col(0)),
            pl.BlockSpec((SUBLANES, 3 * GDN_W), fixed),
            pl.BlockSpec((SUBLANES, LANES), fixed),
            pl.BlockSpec((1, GDN_DV), fixed),
            pl.BlockSpec((ts, ts), fixed),
        ],
        out_specs=pl.BlockSpec((1, ts, GDN_W), col(0)),
        out_shape=jax.ShapeDtypeStruct((b, s, GDN_W), BF16),
        scratch_shapes=[
            pltpu.VMEM((SUBLANES, 3 * GDN_W), F32),
            pltpu.VMEM((ts + SUBLANES, 3 * GDN_W), F32),
            pltpu.VMEM((ts, GDN_W), F32),
            pltpu.VMEM((ts, GDN_W), F32),
            pltpu.VMEM((ts, GDN_W), F32),
            pltpu.VMEM((ts, LANES), F32),
            pltpu.VMEM((ts, LANES), F32),
            pltpu.VMEM((n_mats, c, c), BF16),
            pltpu.VMEM((n_mats, c, c), BF16),
            pltpu.VMEM((GDN_HEADS, GDN_DK, GDN_DV), F32),
        ],
        compiler_params=_cparams(("arbitrary", "arbitrary")),
        name="gdn_branch",
    )(u3, u3, u3, u3, ba3, conv_w, gparams, norm_w, tril)


def _strided_rows(r, n, dil):
    return slice(None) if dil == 1 else pl.ds(r, n, stride=dil)


def _attprep_kernel(q_ref, k_ref, v_ref, c_ref, s1_ref, s2_ref, *rest):
    outs, (qs, ks, vs) = rest[:-3], rest[-3:]
    ts = PREP_TS
    cs, s1, s2 = c_ref[...], s1_ref[...], s2_ref[...]
    for p in range(ATT_W // LANES):
        sl = slice(p * LANES, (p + 1) * LANES)
        for src, dst, scale in ((q_ref, qs, ATT_HD ** -0.5), (k_ref, ks, None)):
            x = src[0, :, sl]
            r = x * cs + pltpu.roll(x, LANES - ROPE_DIM // 2, axis=1) * s1 + pltpu.roll(x, ROPE_DIM // 2, axis=1) * s2
            dst[p] = r if scale is None else r * scale
        vs[p] = v_ref[0, :, sl]
    for di, (_, dil) in enumerate(DIL_PATTERNS):
        n = ts // dil
        for r in range(dil):
            rows = _strided_rows(r, n, dil)
            for p in range(ATT_W // LANES):
                sl = slice(p * LANES, (p + 1) * LANES)
                for dst, src in zip(outs[3 * di:3 * di + 3], (qs, ks, vs)):
                    dst[0, r, :, sl] = src[p, rows, :].astype(BF16)


def _att_prep(u3, cos_t, sin_a, sin_b):
    b, s, _ = u3.shape
    ts = PREP_TS
    col = lambda j: (lambda bi, si: (bi, si, j))
    tab = pl.BlockSpec((ts, LANES), lambda bi, si: (si, 0))
    out_specs, out_shape = [], []
    for _, dil in DIL_PATTERNS:
        out_specs += [pl.BlockSpec((1, dil, ts // dil, ATT_W), lambda bi, si: (bi, 0, si, 0))] * 3
        out_shape += [jax.ShapeDtypeStruct((b, dil, s // dil, ATT_W), BF16)] * 3
    return pl.pallas_call(
        _attprep_kernel,
        grid=(b, s // ts),
        in_specs=[pl.BlockSpec((1, ts, ATT_W), col(COL_AQ)), pl.BlockSpec((1, ts, ATT_W), col(COL_AK)),
                  pl.BlockSpec((1, ts, ATT_W), col(COL_AV)), tab, tab, tab],
        out_specs=out_specs,
        out_shape=out_shape,
        scratch_shapes=[pltpu.VMEM((ATT_W // LANES, ts, LANES), F32)] * 3,
        compiler_params=_cparams(("arbitrary", "arbitrary")),
        name="att_prep",
    )(u3, u3, u3, cos_t, sin_a, sin_b)


def _att_kernel(q_ref, k_ref, v_ref, o_ref, lse_ref, kprev, vprev):
    blk = ATT_BLOCK
    n = pl.program_id(2)

    @pl.when(n == 0)
    def _():
        kprev[...] = jnp.zeros(kprev.shape, BF16)
        vprev[...] = jnp.zeros(vprev.shape, BF16)

    row = lax.broadcasted_iota(jnp.int32, (blk, 2 * blk), 0)
    col = lax.broadcasted_iota(jnp.int32, (blk, 2 * blk), 1)
    first = jnp.where(n > 0, 0, 4 * blk)
    ahead = col - row
    valid = jnp.where(col < blk, ahead - first, blk - ahead) >= 0
    lane = lax.broadcasted_iota(jnp.int32, (blk, LANES), 1)
    low = lane < ATT_HD

    scores = []
    for p in range(ATT_W // LANES):
        sl = slice(p * LANES, (p + 1) * LANES)
        kcat = jnp.concatenate([kprev[:, sl], k_ref[:, sl]], axis=0)
        qp = q_ref[:, sl]
        for hh in range(2):
            hm = low if hh == 0 else jnp.logical_not(low)
            scores.append(_dot_nt(jnp.where(hm, qp, jnp.zeros_like(qp)), kcat))

    lse_tile = jnp.zeros((blk, LANES), F32)
    for p in range(ATT_W // LANES):
        sl = slice(p * LANES, (p + 1) * LANES)
        vcat = jnp.concatenate([vprev[:, sl], v_ref[:, sl]], axis=0)
        o_pair = None
        for hh in range(2):
            sc = jnp.where(valid, scores[2 * p + hh], NEG_INF)
            m = jnp.max(sc, axis=-1, keepdims=True)
            pe = jnp.exp(sc - m)
            l = jnp.sum(pe, axis=-1, keepdims=True)
            o = _dot(pe.astype(BF16), vcat) * (1.0 / l)
            o_pair = o if hh == 0 else jnp.where(low, o_pair, o)
            lse_tile = jnp.where(lane == 2 * p + hh, m + jnp.log(l), lse_tile)
        o_ref[:, sl] = o_pair
    lse_ref[...] = lse_tile
    kprev[...] = k_ref[...]
    vprev[...] = v_ref[...]


def _att_pattern(qd, kd, vd):
    b, dil, lr, _ = qd.shape
    blk = ATT_BLOCK
    idx = lambda bi, r, n: (bi, r, n, 0)
    spec = pl.BlockSpec((None, None, blk, ATT_W), idx)
    return pl.pallas_call(
        _att_kernel,
        grid=(b, dil, lr // blk),
        in_specs=[spec, spec, spec],
        out_specs=[spec, pl.BlockSpec((None, None, blk, LANES), idx)],
        out_shape=[jax.ShapeDtypeStruct((b, dil, lr, ATT_W), F32),
                   jax.ShapeDtypeStruct((b, dil, lr, LANES), F32)],
        scratch_shapes=[pltpu.VMEM((blk, ATT_W), BF16), pltpu.VMEM((blk, ATT_W), BF16)],
        compiler_params=_cparams(("arbitrary", "arbitrary", "arbitrary")),
        name=f"att_dil{dil}",
    )(qd, kd, vd)


def _combine_kernel(o1_ref, o2_ref, o3_ref, l1_ref, l2_ref, l3_ref, g_ref, y_ref, oscr, lscr):
    ts = COMB_TS
    for g, (o_ref, l_ref) in enumerate(((o2_ref, l2_ref), (o3_ref, l3_ref))):
        dil = DIL_PATTERNS[g + 1][1]
        n = ts // dil
        for r in range(dil):
            rows = pl.ds(r, n, stride=dil)
            lscr[g, rows, :] = l_ref[r]
            for p in range(ATT_W // LANES):
                oscr[g, p, rows, :] = o_ref[r, :, p * LANES:(p + 1) * LANES]
    l1, l2, l3 = l1_ref[0], lscr[0], lscr[1]
    mx = jnp.maximum(jnp.maximum(l1, l2), l3)
    e1, e2, e3 = jnp.exp(l1 - mx), jnp.exp(l2 - mx), jnp.exp(l3 - mx)
    inv = 1.0 / (e1 + e2 + e3)
    ws = (e1 * inv, e2 * inv, e3 * inv)
    low = lax.broadcasted_iota(jnp.int32, (ts, LANES), 1) < ATT_HD
    for p in range(ATT_W // LANES):
        sl = slice(p * LANES, (p + 1) * LANES)
        tiles = (o1_ref[0, :, sl], oscr[0, p], oscr[1, p])
        acc = None
        for w, o in zip(ws, tiles):
            t = jnp.where(low, w[:, 2 * p:2 * p + 1], w[:, 2 * p + 1:2 * p + 2]) * o
            acc = t if acc is None else acc + t
        y_ref[0, :, sl] = (acc * _silu(g_ref[0, :, sl])).astype(BF16)


def _att_combine(outs, lses, u3):
    b, s, _ = u3.shape
    ts = COMB_TS
    ospecs = [pl.BlockSpec((None, dil, ts // dil, ATT_W), lambda bi, si: (bi, 0, si, 0)) for _, dil in DIL_PATTERNS]
    lspecs = [pl.BlockSpec((None, dil, ts // dil, LANES), lambda bi, si: (bi, 0, si, 0)) for _, dil in DIL_PATTERNS]
    return pl.pallas_call(
        _combine_kernel,
        grid=(b, s // ts),
        in_specs=ospecs + lspecs + [pl.BlockSpec((1, ts, ATT_W), lambda bi, si: (bi, si, COL_AG))],
        out_specs=pl.BlockSpec((1, ts, ATT_W), lambda bi, si: (bi, si, 0)),
        out_shape=jax.ShapeDtypeStruct((b, s, ATT_W), BF16),
        scratch_shapes=[pltpu.VMEM((2, ATT_W // LANES, ts, LANES), F32), pltpu.VMEM((2, ts, LANES), F32)],
        compiler_params=_cparams(("arbitrary", "arbitrary")),
        name="att_combine",
    )(*outs, *lses, u3)


def _rope_lane_tables(s):
    half = ROPE_DIM // 2
    inv = ROPE_THETA ** (-jnp.arange(half, dtype=F32) / half)
    ang = jnp.arange(s, dtype=F32)[:, None] * inv[None, :]
    cos, sin = jnp.cos(ang), jnp.sin(ang)
    ones = jnp.ones((s, ATT_HD - ROPE_DIM), F32)
    zeros = jnp.zeros((s, ATT_HD - half), F32)
    cos_h = jnp.concatenate([cos, cos, ones], axis=-1)
    sin_a = jnp.concatenate([-sin, zeros], axis=-1)
    sin_b = jnp.concatenate([jnp.zeros((s, half), F32), sin, zeros[:, half:]], axis=-1)
    rep = lambda t: jnp.concatenate([t, t], axis=-1)
    return rep(cos_h), rep(sin_a), rep(sin_b)


def kernel(x, norm_w, w_in, conv_qkv_w, a_log, dt_bias, gdn_norm_w, conf_dw_w, conf_dw_b, conf_ln_w,
           conf_ln_b, conf_pw_w, w_out, final_norm_w):
    b, s, d = x.shape
    depth = w_in.shape[0]
    m = b * s
    assert d == D_MODEL and m % INPROJ_TM == 0 and s % (DIL_PATTERNS[-1][1] * ATT_BLOCK) == 0

    ba0 = 2 * CONV_CH + CONV_CH + 4 * GDN_W
    w_main = jnp.concatenate([w_in[:, :, :ba0], w_in[:, :, ba0 + 2 * GDN_HEADS:]], axis=-1).astype(BF16)
    w_ba = jnp.zeros((depth, d, BA_W), F32)
    w_ba = w_ba.at[:, :, 0:GDN_HEADS].set(w_in[:, :, ba0:ba0 + GDN_HEADS])
    w_ba = w_ba.at[:, :, LANES:LANES + GDN_HEADS].set(w_in[:, :, ba0 + GDN_HEADS:ba0 + 2 * GDN_HEADS])
    w_ba = w_ba.astype(BF16)
    w_out_b = w_out.astype(BF16)
    pw_b = conf_pw_w.astype(BF16)
    dw_w = jnp.pad(conf_dw_w, ((0, 0), (0, 32 - CONV_WIDTH), (0, 0)))
    cw = jnp.pad(conv_qkv_w, ((0, 0), (0, SUBLANES - SHORT_CONV), (0, 0)))
    gparams = jnp.zeros((depth, SUBLANES, LANES), F32)
    gparams = gparams.at[:, 0, 0:GDN_HEADS].set(a_log).at[:, 1, 0:GDN_HEADS].set(dt_bias)
    cos_t, sin_a, sin_b = _rope_lane_tables(s)

    x2 = x.reshape(m, d)
    xn = _pre_norm(x2, norm_w[0][None, :])
    for l in range(depth):
        last = l == depth - 1
        u, ba = _in_proj(xn, w_main[l], w_ba[l])
        u3 = u.reshape(b, s, MAIN_W)
        y_conv = _conv_branch(u3, dw_w[l], conf_dw_b[l][None, :], conf_ln_w[l][None, :], conf_ln_b[l][None, :],
                              pw_b[l])
        y_gdn = _gdn_branch(u3, ba.reshape(b, s, BA_W), cw[l], gparams[l], gdn_norm_w[l][None, :])
        qkv = _att_prep(u3, cos_t, sin_a, sin_b)
        outs, lses = [], []
        for g in range(len(DIL_PATTERNS)):
            o_g, lse_g = _att_pattern(*qkv[3 * g:3 * g + 3])
            outs.append(o_g)
            lses.append(lse_g)
        y_att = _att_combine(outs, lses, u3)
        next_w = final_norm_w if last else norm_w[l + 1]
        res = _out_proj(y_conv.reshape(m, CONV_CH), y_gdn.reshape(m, GDN_W), y_att.reshape(m, ATT_W), x2,
                        w_out_b[l, :CONV_CH], w_out_b[l, CONV_CH:CONV_CH + GDN_W], w_out_b[l, CONV_CH + GDN_W:],
                        next_w[None, :], final=last)
        x2 = res[0]
        xn = None if last else res[1]
    return x2.reshape(b, s, d)
```

```python
import functools

import jax
import jax.numpy as jnp
from jax import lax
from jax.experimental import pallas as pl
from jax.experimental.pallas import tpu as pltpu

F32 = jnp.float32
BF16 = jnp.bfloat16

D_MODEL = 2048
GDN_DK = 128
GDN_DV = 128
GDN_W = 768
GDN_HEADS = 6
ATT_HD = 64
ATT_W = 768
ATT_HEADS = 12
CONV_CH = 512
CONV_WIDTH = 31
SHORT_CONV = 4
GDN_CHUNK = 64
ROPE_THETA = 500000.0
ROPE_DIM = 16
DIL_PATTERNS = ((128, 1), (512, 4), (2048, 16))
ATT_BLOCK = 128
NEG_INF = -1e30
MAIN_W = 2 * CONV_CH + CONV_CH + 4 * GDN_W + 4 * ATT_W
BA_W = 256

COL_GQ, COL_GK, COL_GV, COL_GZ, COL_AQ, COL_AK, COL_AV, COL_AG = 2, 3, 4, 5, 6, 7, 8, 9

LANES = 128
SUBLANES = 8
VMEM_LIMIT = 60 * 1024 * 1024
INPROJ_TM, INPROJ_TN = 2048, 768
OUTPROJ_TM = 512
CONV_TS, CONV_RC, CONV_HALO = 256, 32, 32
GDN_NB, GDN_TS = 4, 128
GDN_TGROUP = 4
ATT_STEP_BLOCKS = 2
PREP_TS = 512
COMB_TS = 512


def _cparams(sem):
    return pltpu.CompilerParams(dimension_semantics=sem, vmem_limit_bytes=VMEM_LIMIT)


def _sigmoid(x):
    return 1.0 / (1.0 + jnp.exp(-x))


def _silu(x):
    return x * _sigmoid(x)


def _dot(a, b):
    return jnp.dot(a, b, preferred_element_type=F32)


def _dot_nt(a, b):
    return lax.dot_general(a, b, (((1,), (1,)), ((), ())), preferred_element_type=F32)


def _rms_norm(x, w):
    ms = jnp.mean(x * x, axis=-1, keepdims=True)
    return (x * lax.rsqrt(ms + 1e-6)) * w


def _prenorm_kernel(x_ref, nw_ref, xn_ref):
    xn_ref[...] = _rms_norm(x_ref[...], nw_ref[...]).astype(BF16)


def _pre_norm(x2, norm_w):
    m = x2.shape[0]
    tm = OUTPROJ_TM
    return pl.pallas_call(
        _prenorm_kernel,
        grid=(m // tm,),
        in_specs=[pl.BlockSpec((tm, D_MODEL), lambda i: (i, 0)), pl.BlockSpec((1, D_MODEL), lambda i: (0, 0))],
        out_specs=pl.BlockSpec((tm, D_MODEL), lambda i: (i, 0)),
        out_shape=jax.ShapeDtypeStruct((m, D_MODEL), BF16),
        compiler_params=_cparams(("arbitrary",)),
        name="pre_norm",
    )(x2, norm_w)


def _inproj_kernel(xn_ref, w_ref, wba_ref, u_ref, ba_ref):
    @pl.when(pl.program_id(1) == 0)
    def _():
        ba_ref[...] = _dot(xn_ref[...], wba_ref[...])

    u_ref[...] = _dot(xn_ref[...], w_ref[...])


def _in_proj(xn, w_main, w_ba):
    m = xn.shape[0]
    tm, tn = INPROJ_TM, INPROJ_TN
    return pl.pallas_call(
        _inproj_kernel,
        grid=(m // tm, MAIN_W // tn),
        in_specs=[
            pl.BlockSpec((tm, D_MODEL), lambda i, j: (i, 0)),
            pl.BlockSpec((D_MODEL, tn), lambda i, j: (0, j)),
            pl.BlockSpec((D_MODEL, BA_W), lambda i, j: (0, 0)),
        ],
        out_specs=[
            pl.BlockSpec((tm, tn), lambda i, j: (i, j)),
            pl.BlockSpec((tm, BA_W), lambda i, j: (i, 0)),
        ],
        out_shape=[jax.ShapeDtypeStruct((m, MAIN_W), F32), jax.ShapeDtypeStruct((m, BA_W), F32)],
        compiler_params=_cparams(("arbitrary", "arbitrary")),
        name="in_proj",
    )(xn, w_main, w_ba)


def _outproj_kernel(yc_ref, yg_ref, ya_ref, x_ref, wc_ref, wg_ref, wa_ref, nw_ref, o_ref, *maybe_xn_ref, final):
    acc = _dot(yc_ref[...], wc_ref[...]) + _dot(yg_ref[...], wg_ref[...]) + _dot(ya_ref[...], wa_ref[...])
    xo = x_ref[...] + acc
    if final:
        o_ref[...] = _rms_norm(xo, nw_ref[...])
    else:
        o_ref[...] = xo
        maybe_xn_ref[0][...] = _rms_norm(xo, nw_ref[...]).astype(BF16)


def _out_proj(y_conv, y_gdn, y_att, x2, w_c, w_g, w_a, next_norm_w, final):
    m = x2.shape[0]
    tm = OUTPROJ_TM
    row = lambda i: (i, 0)
    fixed = lambda i: (0, 0)
    out_specs = [pl.BlockSpec((tm, D_MODEL), row)]
    out_shape = [jax.ShapeDtypeStruct((m, D_MODEL), F32)]
    if not final:
        out_specs.append(pl.BlockSpec((tm, D_MODEL), row))
        out_shape.append(jax.ShapeDtypeStruct((m, D_MODEL), BF16))
    return pl.pallas_call(
        functools.partial(_outproj_kernel, final=final),
        grid=(m // tm,),
        in_specs=[
            pl.BlockSpec((tm, CONV_CH), row),
            pl.BlockSpec((tm, GDN_W), row),
            pl.BlockSpec((tm, ATT_W), row),
            pl.BlockSpec((tm, D_MODEL), row),
            pl.BlockSpec((CONV_CH, D_MODEL), fixed),
            pl.BlockSpec((GDN_W, D_MODEL), fixed),
            pl.BlockSpec((ATT_W, D_MODEL), fixed),
            pl.BlockSpec((1, D_MODEL), fixed),
        ],
        out_specs=out_specs,
        out_shape=out_shape,
        compiler_params=_cparams(("arbitrary",)),
        name="out_proj_final" if final else "out_proj",
    )(y_conv, y_gdn, y_att, x2, w_c, w_g, w_a, next_norm_w)


def _conv_kernel(cin_ref, cgate_ref, dww_ref, dwb_ref, lnw_ref, lnb_ref, pw_ref, y_ref, hbuf, hsh, act):
    ts, rc, halo = CONV_TS, CONV_RC, CONV_HALO
    s = pl.program_id(1)

    @pl.when(s == 0)
    def _():
        hbuf[0:halo, :] = jnp.zeros((halo, CONV_CH), F32)

    @pl.when(s > 0)
    def _():
        hbuf[0:halo, :] = hbuf[ts:ts + halo, :]

    u = cin_ref[0]
    hbuf[halo:halo + ts, :] = u[:, :CONV_CH] * _sigmoid(u[:, CONV_CH:])

    span = ts + halo - SUBLANES
    for r in range(1, SUBLANES):
        hsh[r - 1, 0:span, :] = hbuf[r:r + span, :]

    base = halo - (CONV_WIDTH - 1)

    for c in range(ts // rc):
        r0 = c * rc
        acc = jnp.zeros((rc, CONV_CH), F32) + dwb_ref[...]
        for k in range(CONV_WIDTH):
            shift = (base + k) % SUBLANES
            lo = r0 + base + k - shift
            rows = hbuf[lo:lo + rc, :] if shift == 0 else hsh[shift - 1, lo:lo + rc, :]
            acc = acc + dww_ref[k:k + 1, :] * rows
        mu = jnp.mean(acc, axis=-1, keepdims=True)
        d = acc - mu
        var = jnp.mean(d * d, axis=-1, keepdims=True)
        hn = d * lax.rsqrt(var + 1e-5) * lnw_ref[...] + lnb_ref[...]
        act[r0:r0 + rc, :] = _silu(hn).astype(BF16)

    y = _dot(act[...], pw_ref[...]) * _silu(cgate_ref[0])
    y_ref[0] = y.astype(BF16)


def _conv_branch(u3, dw_w, dw_b, ln_w, ln_b, pw_w):
    b, s, _ = u3.shape
    ts = CONV_TS
    fixed = lambda bi, si: (0, 0)
    return pl.pallas_call(
        _conv_kernel,
        grid=(b, s // ts),
        in_specs=[
            pl.BlockSpec((1, ts, 2 * CONV_CH), lambda bi, si: (bi, si, 0)),
            pl.BlockSpec((1, ts, CONV_CH), lambda bi, si: (bi, si, 2)),
            pl.BlockSpec((32, CONV_CH), fixed),
            pl.BlockSpec((1, CONV_CH), fixed),
            pl.BlockSpec((1, CONV_CH), fixed),
            pl.BlockSpec((1, CONV_CH), fixed),
            pl.BlockSpec((CONV_CH, CONV_CH), fixed),
        ],
        out_specs=pl.BlockSpec((1, ts, CONV_CH), lambda bi, si: (bi, si, 0)),
        out_shape=jax.ShapeDtypeStruct((b, s, CONV_CH), BF16),
        scratch_shapes=[pltpu.VMEM((ts + CONV_HALO, CONV_CH), F32),
                        pltpu.VMEM((SUBLANES - 1, ts + CONV_HALO, CONV_CH), F32),
                        pltpu.VMEM((ts, CONV_CH), BF16)],
        compiler_params=_cparams(("arbitrary", "arbitrary")),
        name="conv_branch",
    )(u3, u3, dw_w, dw_b, ln_w, ln_b, pw_w)


def _split3(x):
    hi = x.astype(BF16)
    r1 = x - hi.astype(F32)
    mid = r1.astype(BF16)
    lo = (r1 - mid.astype(F32)).astype(BF16)
    return hi, mid, lo


def _unit_lower_inverse_many(a_list, ii, jj):
    c = a_list[0].shape[0]
    same = lambda log2: jnp.right_shift(ii, log2) == jnp.right_shift(jj, log2)
    in_blk = same(1)
    rs = [-jnp.where(in_blk, a, 0.0) for a in a_list]
    lg = 1
    while (1 << lg) < c:
        in_pair = same(lg + 1)
        es = [jnp.where(in_pair, jnp.where(in_blk, 0.0, a), 0.0) for a in a_list]
        rbs = [r.astype(BF16) for r in rs]
        xs = [_dot(rb, e.astype(BF16)) for rb, e in zip(rbs, es)]
        ys = [_dot((e + x).astype(BF16), rb) for e, x, rb in zip(es, xs, rbs)]
        rs = [r - e - x - y for r, e, x, y in zip(rs, es, xs, ys)]
        in_blk = in_pair
        lg += 1
    return rs


def _gdn_kernel(q_ref, k_ref, v_ref, z_ref, ba_ref, cw_ref, gp_ref, nw_ref, tril_ref, y_ref,
                carry, xbuf, qn, kn, vn, gcum, beta, tbuf, qkbuf, kdbuf, obuf, state):
    nb, ts, c = GDN_NB, GDN_TS, GDN_CHUNK
    n_chunks = ts // c
    w3 = 3 * GDN_W
    heads = range(GDN_HEADS)
    hsl = lambda h: slice(h * GDN_DK, (h + 1) * GDN_DK)
    s = pl.program_id(1)

    @pl.when(s == 0)
    def _():
        carry[...] = jnp.zeros(carry.shape, F32)
        state[...] = jnp.zeros(state.shape, F32)

    base = SUBLANES - (SHORT_CONV - 1)
    gsteps = []
    for bb in range(nb):
        rows_b = slice(bb * ts, (bb + 1) * ts)
        xbuf[bb, 0:SUBLANES, :] = carry[bb]
        xbuf[bb, SUBLANES:SUBLANES + ts, 0:GDN_W] = q_ref[bb]
        xbuf[bb, SUBLANES:SUBLANES + ts, GDN_W:2 * GDN_W] = k_ref[bb]
        xbuf[bb, SUBLANES:SUBLANES + ts, 2 * GDN_W:w3] = v_ref[bb]
        carry[bb] = xbuf[bb, ts:ts + SUBLANES, :]
        for part, dst in ((0, qn), (1, kn), (2, vn)):
            for h in heads:
                lo = part * GDN_W + h * GDN_DK
                acc = jnp.zeros((ts, GDN_DK), F32)
                for t in range(SHORT_CONV):
                    acc = acc + cw_ref[t:t + 1, lo:lo + GDN_DK] * xbuf[bb, base + t:base + t + ts, lo:lo + GDN_DK]
                acc = _silu(acc)
                if part == 0:
                    acc = (acc * lax.rsqrt(jnp.sum(acc * acc, axis=-1, keepdims=True) + 1e-6)) * (GDN_DK ** -0.5)
                elif part == 1:
                    acc = acc * lax.rsqrt(jnp.sum(acc * acc, axis=-1, keepdims=True) + 1e-6)
                dst[rows_b, hsl(h)] = acc

        ba = ba_ref[bb]
        beta[rows_b, :] = _sigmoid(ba[:, 0:LANES])
        xa = ba[:, LANES:2 * LANES] + gp_ref[1:2, :]
        softplus = jnp.maximum(xa, 0.0) + jnp.log(1.0 + jnp.exp(-jnp.abs(xa)))
        gsteps.append(-jnp.exp(gp_ref[0:1, :]) * softplus)

    tril = tril_ref[...]
    g_hi, g_mid, g_lo = _split3(jnp.concatenate(gsteps, axis=0))
    gcum[...] = _dot(tril, g_hi) + _dot(tril, g_mid) + _dot(tril, g_lo)

    ii = lax.broadcasted_iota(jnp.int32, (c, c), 0)
    jj = lax.broadcasted_iota(jnp.int32, (c, c), 1)
    causal = ii >= jj
    strict = ii > jj
    eye = jnp.where(ii == jj, 1.0, 0.0)

    def t_body(gi, _):
        items = []
        for cc in range(GDN_TGROUP):
            ci_ = gi * GDN_TGROUP + cc
            r0 = pl.multiple_of(ci_ * c, c)
            g_blk = gcum[pl.ds(r0, c), :]
            g_t = g_blk.T
            b_blk = beta[pl.ds(r0, c), :]
            for h in heads:
                items.append((ci_ * GDN_HEADS + h, r0, h, g_blk[:, h:h + 1], g_t[h:h + 1, :], b_blk[:, h:h + 1]))
        khs = [kn[pl.ds(r0, c), hsl(h)] for _, r0, h, _, _, _ in items]
        for it, kh in zip(items, khs):
            gr = it[4]
            kdbuf[it[0]] = (kh.T * jnp.exp(gr[:, c - 1:c] - gr)).astype(BF16)
        kbs = [kh.astype(BF16) for kh in khs]
        qbs = [qn[pl.ds(r0, c), hsl(h)].astype(BF16) for _, r0, h, _, _, _ in items]
        decays = [jnp.where(causal, jnp.exp(jnp.where(causal, gc - gr, 0.0)), 0.0) for _, _, _, gc, gr, _ in items]
        kks = [_dot_nt(kb, kb) for kb in kbs]
        qks = [_dot_nt(qb, kb) for qb, kb in zip(qbs, kbs)]
        a_s = [jnp.where(strict, it[5] * kk * dc, 0.0) for it, kk, dc in zip(items, kks, decays)]
        for it, qk, dc in zip(items, qks, decays):
            qkbuf[it[0]] = jnp.where(causal, qk * dc, 0.0).astype(BF16)
        rs = _unit_lower_inverse_many(a_s, ii, jj)
        for it, r in zip(items, rs):
            tbuf[it[0]] = (r + eye).astype(BF16)
        return 0

    lax.fori_loop(0, nb * n_chunks // GDN_TGROUP, t_body, 0)

    chains = [(bb, h) for bb in range(nb) for h in heads]

    def chunk_body(ci_, _):
        rows, mats, gcs, bcs = [], [], [], []
        for bb in range(nb):
            unit = bb * n_chunks + ci_
            rows_u = pl.ds(pl.multiple_of(unit * c, c), c)
            g_blk = gcum[rows_u, :]
            b_blk = beta[rows_u, :]
            for h in heads:
                rows.append(rows_u)
                mats.append(unit * GDN_HEADS + h)
                gcs.append(g_blk[:, h:h + 1])
                bcs.append(b_blk[:, h:h + 1])
        n = range(len(chains))
        egs = [jnp.exp(gc) for gc in gcs]
        glasts = [gc[c - 1:c, :] for gc in gcs]
        s_hs = [state[bb * GDN_HEADS + h] for bb, h in chains]
        s_bs = [sh.astype(BF16) for sh in s_hs]
        kss = [_dot(kn[rows[i], hsl(chains[i][1])].astype(BF16), s_bs[i]) for i in n]
        o_state = [_dot((qn[rows[i], hsl(chains[i][1])] * egs[i]).astype(BF16), s_bs[i]) for i in n]
        us = [bcs[i] * (vn[rows[i], hsl(chains[i][1])] - egs[i] * kss[i]) for i in n]
        vbs = [_dot(tbuf[mats[i]], us[i].astype(BF16)).astype(BF16) for i in n]
        for i, (bb, h) in enumerate(chains):
            state[bb * GDN_HEADS + h] = s_hs[i] * jnp.exp(glasts[i]) + _dot(kdbuf[mats[i]], vbs[i])
        for i, (bb, h) in enumerate(chains):
            obuf[rows[i], hsl(h)] = o_state[i] + _dot(qkbuf[mats[i]], vbs[i])
        return 0

    lax.fori_loop(0, n_chunks, chunk_body, 0)

    for bb in range(nb):
        for h in heads:
            o = obuf[bb * ts:(bb + 1) * ts, hsl(h)]
            on = (o * lax.rsqrt(jnp.mean(o * o, axis=-1, keepdims=True) + 1e-6)) * nw_ref[...]
            y_ref[bb, :, hsl(h)] = (on * _silu(z_ref[bb, :, hsl(h)])).astype(BF16)


def _gdn_branch(u3, ba3, conv_w, gparams, norm_w):
    b, s, _ = u3.shape
    nb, ts, c = GDN_NB, GDN_TS, GDN_CHUNK
    assert b % nb == 0 and (nb * ts // c) % GDN_TGROUP == 0
    fixed = lambda bi, si: (0, 0)
    col = lambda j: (lambda bi, si: (bi, si, j))
    rows = nb * ts
    ri = lax.broadcasted_iota(jnp.int32, (rows, rows), 0)
    ci = lax.broadcasted_iota(jnp.int32, (rows, rows), 1)
    tril = ((ri // c == ci // c) & (ri >= ci)).astype(BF16)
    n_mats = (rows // c) * GDN_HEADS
    return pl.pallas_call(
        _gdn_kernel,
        grid=(b // nb, s // ts),
        in_specs=[
            pl.BlockSpec((nb, ts, GDN_W), col(COL_GQ)),
            pl.BlockSpec((nb, ts, GDN_W), col(COL_GK)),
            pl.BlockSpec((nb, ts, GDN_W), col(COL_GV)),
            pl.BlockSpec((nb, ts, GDN_W), col(COL_GZ)),
            pl.BlockSpec((nb, ts, BA_W), col(0)),
            pl.BlockSpec((SUBLANES, 3 * GDN_W), fixed),
            pl.BlockSpec((SUBLANES, LANES), fixed),
            pl.BlockSpec((1, GDN_DV), fixed),
            pl.BlockSpec((rows, rows), fixed),
        ],
        out_specs=pl.BlockSpec((nb, ts, GDN_W), col(0)),
        out_shape=jax.ShapeDtypeStruct((b, s, GDN_W), BF16),
        scratch_shapes=[
            pltpu.VMEM((nb, SUBLANES, 3 * GDN_W), F32),
            pltpu.VMEM((nb, ts + SUBLANES, 3 * GDN_W), F32),
            pltpu.VMEM((rows, GDN_W), F32),
            pltpu.VMEM((rows, GDN_W), F32),
            pltpu.VMEM((rows, GDN_W), F32),
            pltpu.VMEM((rows, LANES), F32),
            pltpu.VMEM((rows, LANES), F32),
            pltpu.VMEM((n_mats, c, c), BF16),
            pltpu.VMEM((n_mats, c, c), BF16),
            pltpu.VMEM((n_mats, GDN_DK, c), BF16),
            pltpu.VMEM((rows, GDN_W), F32),
            pltpu.VMEM((nb * GDN_HEADS, GDN_DK, GDN_DV), F32),
        ],
        compiler_params=_cparams(("arbitrary", "arbitrary")),
        name="gdn_branch",
    )(u3, u3, u3, u3, ba3, conv_w, gparams, norm_w, tril)


def _strided_rows(r, n, dil):
    return slice(None) if dil == 1 else pl.ds(r, n, stride=dil)


def _attprep_kernel(q_ref, k_ref, v_ref, c_ref, s1_ref, s2_ref, *rest):
    outs, (qs, ks, vs) = rest[:-3], rest[-3:]
    ts = PREP_TS
    cs, s1, s2 = c_ref[...], s1_ref[...], s2_ref[...]
    for p in range(ATT_W // LANES):
        sl = slice(p * LANES, (p + 1) * LANES)
        for src, dst, scale in ((q_ref, qs, ATT_HD ** -0.5), (k_ref, ks, None)):
            x = src[0, :, sl]
            r = x * cs + pltpu.roll(x, LANES - ROPE_DIM // 2, axis=1) * s1 + pltpu.roll(x, ROPE_DIM // 2, axis=1) * s2
            dst[p] = r if scale is None else r * scale
        vs[p] = v_ref[0, :, sl]
    for di, (_, dil) in enumerate(DIL_PATTERNS):
        n = ts // dil
        for r in range(dil):
            rows = _strided_rows(r, n, dil)
            for p in range(ATT_W // LANES):
                sl = slice(p * LANES, (p + 1) * LANES)
                for dst, src in zip(outs[3 * di:3 * di + 3], (qs, ks, vs)):
                    dst[0, r, :, sl] = src[p, rows, :].astype(BF16)


def _att_prep(u3, cos_t, sin_a, sin_b):
    b, s, _ = u3.shape
    ts = PREP_TS
    col = lambda j: (lambda bi, si: (bi, si, j))
    tab = pl.BlockSpec((ts, LANES), lambda bi, si: (si, 0))
    out_specs, out_shape = [], []
    for _, dil in DIL_PATTERNS:
        out_specs += [pl.BlockSpec((1, dil, ts // dil, ATT_W), lambda bi, si: (bi, 0, si, 0))] * 3
        out_shape += [jax.ShapeDtypeStruct((b, dil, s // dil, ATT_W), BF16)] * 3
    return pl.pallas_call(
        _attprep_kernel,
        grid=(b, s // ts),
        in_specs=[pl.BlockSpec((1, ts, ATT_W), col(COL_AQ)), pl.BlockSpec((1, ts, ATT_W), col(COL_AK)),
                  pl.BlockSpec((1, ts, ATT_W), col(COL_AV)), tab, tab, tab],
        out_specs=out_specs,
        out_shape=out_shape,
        scratch_shapes=[pltpu.VMEM((ATT_W // LANES, ts, LANES), F32)] * 3,
        compiler_params=_cparams(("arbitrary", "arbitrary")),
        name="att_prep",
    )(u3, u3, u3, cos_t, sin_a, sin_b)


def _att_kernel(q_ref, k_ref, v_ref, o_ref, lse_ref, kprev, vprev):
    blk = ATT_BLOCK
    n = pl.program_id(2)

    @pl.when(n == 0)
    def _():
        kprev[...] = jnp.zeros(kprev.shape, BF16)
        vprev[...] = jnp.zeros(vprev.shape, BF16)

    row = lax.broadcasted_iota(jnp.int32, (blk, 2 * blk), 0)
    col = lax.broadcasted_iota(jnp.int32, (blk, 2 * blk), 1)
    first = jnp.where(n > 0, 0, 4 * blk)
    ahead = col - row
    lane = lax.broadcasted_iota(jnp.int32, (blk, LANES), 1)
    low = lane < ATT_HD

    for j in range(ATT_STEP_BLOCKS):
        rows = slice(j * blk, (j + 1) * blk)
        if j == 0:
            k_before, v_before = kprev, vprev
            valid = jnp.where(col < blk, ahead - first, blk - ahead) >= 0
        else:
            k_before, v_before = k_ref.at[(j - 1) * blk:j * blk], v_ref.at[(j - 1) * blk:j * blk]
            valid = jnp.where(col < blk, ahead, blk - ahead) >= 0

        scores = []
        for p in range(ATT_W // LANES):
            sl = slice(p * LANES, (p + 1) * LANES)
            kcat = jnp.concatenate([k_before[:, sl], k_ref[rows, sl]], axis=0)
            qp = q_ref[rows, sl]
            for hh in range(2):
                hm = low if hh == 0 else jnp.logical_not(low)
                scores.append(_dot_nt(jnp.where(hm, qp, jnp.zeros_like(qp)), kcat))

        lse_tile = jnp.zeros((blk, LANES), F32)
        for p in range(ATT_W // LANES):
            sl = slice(p * LANES, (p + 1) * LANES)
            vcat = jnp.concatenate([v_before[:, sl], v_ref[rows, sl]], axis=0)
            o_pair = None
            for hh in range(2):
                sc = jnp.where(valid, scores[2 * p + hh], NEG_INF)
                m = jnp.max(sc, axis=-1, keepdims=True)
                pe = jnp.exp(sc - m)
                l = jnp.sum(pe, axis=-1, keepdims=True)
                o = _dot(pe.astype(BF16), vcat) * (1.0 / l)
                o_pair = o if hh == 0 else jnp.where(low, o_pair, o)
                lse_tile = jnp.where(lane == 2 * p + hh, m + jnp.log(l), lse_tile)
            o_ref[rows, sl] = o_pair
        lse_ref[rows, :] = lse_tile

    last = slice((ATT_STEP_BLOCKS - 1) * blk, ATT_STEP_BLOCKS * blk)
    kprev[...] = k_ref[last, :]
    vprev[...] = v_ref[last, :]


def _att_pattern(qd, kd, vd):
    b, dil, lr, _ = qd.shape
    blk = ATT_STEP_BLOCKS * ATT_BLOCK
    idx = lambda bi, r, n: (bi, r, n, 0)
    spec = pl.BlockSpec((None, None, blk, ATT_W), idx)
    return pl.pallas_call(
        _att_kernel,
        grid=(b, dil, lr // blk),
        in_specs=[spec, spec, spec],
        out_specs=[spec, pl.BlockSpec((None, None, blk, LANES), idx)],
        out_shape=[jax.ShapeDtypeStruct((b, dil, lr, ATT_W), F32),
                   jax.ShapeDtypeStruct((b, dil, lr, LANES), F32)],
        scratch_shapes=[pltpu.VMEM((ATT_BLOCK, ATT_W), BF16), pltpu.VMEM((ATT_BLOCK, ATT_W), BF16)],
        compiler_params=_cparams(("arbitrary", "arbitrary", "arbitrary")),
        name=f"att_dil{dil}",
    )(qd, kd, vd)


def _combine_kernel(o1_ref, o2_ref, o3_ref, l1_ref, l2_ref, l3_ref, g_ref, e_ref, y_ref, oscr, lscr):
    ts = COMB_TS
    for g, (o_ref, l_ref) in enumerate(((o2_ref, l2_ref), (o3_ref, l3_ref))):
        dil = DIL_PATTERNS[g + 1][1]
        n = ts // dil
        for r in range(dil):
            rows = pl.ds(r, n, stride=dil)
            lscr[g, rows, :] = l_ref[r]
            for p in range(ATT_W // LANES):
                oscr[g, p, rows, :] = o_ref[r, :, p * LANES:(p + 1) * LANES]
    l1, l2, l3 = l1_ref[0], lscr[0], lscr[1]
    mx = jnp.maximum(jnp.maximum(l1, l2), l3)
    e1, e2, e3 = jnp.exp(l1 - mx), jnp.exp(l2 - mx), jnp.exp(l3 - mx)
    inv = 1.0 / (e1 + e2 + e3)
    spread = e_ref[...]
    ws = []
    for w in (e1 * inv, e2 * inv, e3 * inv):
        hi = w.astype(BF16)
        lo = (w - hi.astype(F32)).astype(BF16)
        ws.append(_dot(jnp.concatenate([hi, lo], axis=1), spread))
    for p in range(ATT_W // LANES):
        sl = slice(p * LANES, (p + 1) * LANES)
        tiles = (o1_ref[0, :, sl], oscr[0, p], oscr[1, p])
        acc = None
        for w, o in zip(ws, tiles):
            t = w[:, sl] * o
            acc = t if acc is None else acc + t
        y_ref[0, :, sl] = (acc * _silu(g_ref[0, :, sl])).astype(BF16)


def _att_combine(outs, lses, u3):
    b, s, _ = u3.shape
    ts = COMB_TS
    ospecs = [pl.BlockSpec((None, dil, ts // dil, ATT_W), lambda bi, si: (bi, 0, si, 0)) for _, dil in DIL_PATTERNS]
    lspecs = [pl.BlockSpec((None, dil, ts // dil, LANES), lambda bi, si: (bi, 0, si, 0)) for _, dil in DIL_PATTERNS]
    head_of_lane = lax.broadcasted_iota(jnp.int32, (2 * LANES, ATT_W), 1) // ATT_HD
    spread = (head_of_lane == lax.broadcasted_iota(jnp.int32, (2 * LANES, ATT_W), 0) % LANES).astype(BF16)
    return pl.pallas_call(
        _combine_kernel,
        grid=(b, s // ts),
        in_specs=ospecs + lspecs + [pl.BlockSpec((1, ts, ATT_W), lambda bi, si: (bi, si, COL_AG)),
                                    pl.BlockSpec((2 * LANES, ATT_W), lambda bi, si: (0, 0))],
        out_specs=pl.BlockSpec((1, ts, ATT_W), lambda bi, si: (bi, si, 0)),
        out_shape=jax.ShapeDtypeStruct((b, s, ATT_W), BF16),
        scratch_shapes=[pltpu.VMEM((2, ATT_W // LANES, ts, LANES), F32), pltpu.VMEM((2, ts, LANES), F32)],
        compiler_params=_cparams(("arbitrary", "arbitrary")),
        name="att_combine",
    )(*outs, *lses, u3, spread)


def _rope_lane_tables(s):
    half = ROPE_DIM // 2
    inv = ROPE_THETA ** (-jnp.arange(half, dtype=F32) / half)
    ang = jnp.arange(s, dtype=F32)[:, None] * inv[None, :]
    cos, sin = jnp.cos(ang), jnp.sin(ang)
    ones = jnp.ones((s, ATT_HD - ROPE_DIM), F32)
    zeros = jnp.zeros((s, ATT_HD - half), F32)
    cos_h = jnp.concatenate([cos, cos, ones], axis=-1)
    sin_a = jnp.concatenate([-sin, zeros], axis=-1)
    sin_b = jnp.concatenate([jnp.zeros((s, half), F32), sin, zeros[:, half:]], axis=-1)
    rep = lambda t: jnp.concatenate([t, t], axis=-1)
    return rep(cos_h), rep(sin_a), rep(sin_b)


def kernel(x, norm_w, w_in, conv_qkv_w, a_log, dt_bias, gdn_norm_w, conf_dw_w, conf_dw_b, conf_ln_w,
           conf_ln_b, conf_pw_w, w_out, final_norm_w):
    b, s, d = x.shape
    depth = w_in.shape[0]
    m = b * s
    assert d == D_MODEL and m % INPROJ_TM == 0 and s % (DIL_PATTERNS[-1][1] * ATT_BLOCK * ATT_STEP_BLOCKS) == 0

    ba0 = 2 * CONV_CH + CONV_CH + 4 * GDN_W
    w_main = jnp.concatenate([w_in[:, :, :ba0], w_in[:, :, ba0 + 2 * GDN_HEADS:]], axis=-1).astype(BF16)
    w_ba = jnp.zeros((depth, d, BA_W), F32)
    w_ba = w_ba.at[:, :, 0:GDN_HEADS].set(w_in[:, :, ba0:ba0 + GDN_HEADS])
    w_ba = w_ba.at[:, :, LANES:LANES + GDN_HEADS].set(w_in[:, :, ba0 + GDN_HEADS:ba0 + 2 * GDN_HEADS])
    w_ba = w_ba.astype(BF16)
    w_out_b = w_out.astype(BF16)
    pw_b = conf_pw_w.astype(BF16)
    dw_w = jnp.pad(conf_dw_w, ((0, 0), (0, 32 - CONV_WIDTH), (0, 0)))
    cw = jnp.pad(conv_qkv_w, ((0, 0), (0, SUBLANES - SHORT_CONV), (0, 0)))
    gparams = jnp.zeros((depth, SUBLANES, LANES), F32)
    gparams = gparams.at[:, 0, 0:GDN_HEADS].set(a_log).at[:, 1, 0:GDN_HEADS].set(dt_bias)
    cos_t, sin_a, sin_b = _rope_lane_tables(s)

    x2 = x.reshape(m, d)
    xn = _pre_norm(x2, norm_w[0][None, :])
    for l in range(depth):
        last = l == depth - 1
        u, ba = _in_proj(xn, w_main[l], w_ba[l])
        u3 = u.reshape(b, s, MAIN_W)
        y_conv = _conv_branch(u3, dw_w[l], conf_dw_b[l][None, :], conf_ln_w[l][None, :], conf_ln_b[l][None, :],
                              pw_b[l])
        y_gdn = _gdn_branch(u3, ba.reshape(b, s, BA_W), cw[l], gparams[l], gdn_norm_w[l][None, :])
        qkv = _att_prep(u3, cos_t, sin_a, sin_b)
        outs, lses = [], []
        for g in range(len(DIL_PATTERNS)):
            o_g, lse_g = _att_pattern(*qkv[3 * g:3 * g + 3])
            outs.append(o_g)
            lses.append(lse_g)
        y_att = _att_combine(outs, lses, u3)
        next_w = final_norm_w if last else norm_w[l + 1]
        res = _out_proj(y_conv.reshape(m, CONV_CH), y_gdn.reshape(m, GDN_W), y_att.reshape(m, ATT_W), x2,
                        w_out_b[l, :CONV_CH], w_out_b[l, CONV_CH:CONV_CH + GDN_W], w_out_b[l, CONV_CH + GDN_W:],
                        next_w[None, :], final=last)
        x2 = res[0]
        xn = None if last else res[1]
    return x2.reshape(b, s, d)
```

```python
import functools

import jax
import jax.numpy as jnp
from jax import lax
from jax.experimental import pallas as pl
from jax.experimental.pallas import tpu as pltpu

F32 = jnp.float32
BF16 = jnp.bfloat16

D_MODEL = 2048
GDN_DK = 128
GDN_DV = 128
GDN_W = 768
GDN_HEADS = 6
ATT_HD = 64
ATT_W = 768
ATT_HEADS = 12
CONV_CH = 512
CONV_WIDTH = 31
SHORT_CONV = 4
GDN_CHUNK = 64
ROPE_THETA = 500000.0
ROPE_DIM = 16
DIL_PATTERNS = ((128, 1), (512, 4), (2048, 16))
ATT_BLOCK = 128
NEG_INF = -1e30
LOG2E = 1.4426950408889634
LN2 = 0.6931471805599453
MAIN_W = 2 * CONV_CH + CONV_CH + 4 * GDN_W + 4 * ATT_W
BA_W = 256

COL_GQ, COL_GK, COL_GV, COL_GZ, COL_AQ, COL_AK, COL_AV, COL_AG = 2, 3, 4, 5, 6, 7, 8, 9

LANES = 128
SUBLANES = 8
VMEM_LIMIT = 60 * 1024 * 1024
INPROJ_TM, INPROJ_TN = 2048, 768
OUTPROJ_TM = 512
CONV_TS, CONV_RC, CONV_HALO = 256, 32, 32
GDN_NB, GDN_TS = 4, 128
GDN_TGROUP = 4
ATT_STEP_BLOCKS = 4
PREP_TS = 512


def _cparams(sem):
    return pltpu.CompilerParams(dimension_semantics=sem, vmem_limit_bytes=VMEM_LIMIT)


def _sigmoid(x):
    return 1.0 / (1.0 + jnp.exp(-x))


def _silu(x):
    return x * _sigmoid(x)


def _dot(a, b):
    return jnp.dot(a, b, preferred_element_type=F32)


def _dot_nt(a, b):
    return lax.dot_general(a, b, (((1,), (1,)), ((), ())), preferred_element_type=F32)


def _rms_norm(x, w):
    ms = jnp.mean(x * x, axis=-1, keepdims=True)
    return (x * lax.rsqrt(ms + 1e-6)) * w


def _prenorm_kernel(x_ref, nw_ref, xn_ref):
    xn_ref[...] = _rms_norm(x_ref[...], nw_ref[...]).astype(BF16)


def _pre_norm(x2, norm_w):
    m = x2.shape[0]
    tm = OUTPROJ_TM
    return pl.pallas_call(
        _prenorm_kernel,
        grid=(m // tm,),
        in_specs=[pl.BlockSpec((tm, D_MODEL), lambda i: (i, 0)), pl.BlockSpec((1, D_MODEL), lambda i: (0, 0))],
        out_specs=pl.BlockSpec((tm, D_MODEL), lambda i: (i, 0)),
        out_shape=jax.ShapeDtypeStruct((m, D_MODEL), BF16),
        compiler_params=_cparams(("arbitrary",)),
        name="pre_norm",
    )(x2, norm_w)


def _inproj_kernel(xn_ref, w_ref, wba_ref, u_ref, ba_ref):
    @pl.when(pl.program_id(1) == 0)
    def _():
        ba_ref[...] = _dot(xn_ref[...], wba_ref[...])

    u_ref[...] = _dot(xn_ref[...], w_ref[...])


def _in_proj(xn, w_main, w_ba, layer):
    m = xn.shape[0]
    tm, tn = INPROJ_TM, INPROJ_TN
    return pl.pallas_call(
        _inproj_kernel,
        grid=(m // tm, MAIN_W // tn),
        in_specs=[
            pl.BlockSpec((tm, D_MODEL), lambda i, j: (i, 0)),
            pl.BlockSpec((None, D_MODEL, tn), lambda i, j: (layer, 0, j)),
            pl.BlockSpec((None, D_MODEL, BA_W), lambda i, j: (layer, 0, 0)),
        ],
        out_specs=[
            pl.BlockSpec((tm, tn), lambda i, j: (i, j)),
            pl.BlockSpec((tm, BA_W), lambda i, j: (i, 0)),
        ],
        out_shape=[jax.ShapeDtypeStruct((m, MAIN_W), F32), jax.ShapeDtypeStruct((m, BA_W), F32)],
        compiler_params=_cparams(("arbitrary", "arbitrary")),
        name="in_proj",
    )(xn, w_main, w_ba)


def _conv_kernel(cin_ref, cgate_ref, dww_ref, dwb_ref, lnw_ref, lnb_ref, pw_ref, y_ref, hbuf, hsh, act):
    ts, rc, halo = CONV_TS, CONV_RC, CONV_HALO
    s = pl.program_id(1)

    @pl.when(s == 0)
    def _():
        hbuf[0:halo, :] = jnp.zeros((halo, CONV_CH), F32)

    @pl.when(s > 0)
    def _():
        hbuf[0:halo, :] = hbuf[ts:ts + halo, :]

    u = cin_ref[0]
    hbuf[halo:halo + ts, :] = u[:, :CONV_CH] * _sigmoid(u[:, CONV_CH:])

    span = ts + halo - SUBLANES
    for r in range(1, SUBLANES):
        hsh[r - 1, 0:span, :] = hbuf[r:r + span, :]

    base = halo - (CONV_WIDTH - 1)

    for c in range(ts // rc):
        r0 = c * rc
        acc = jnp.zeros((rc, CONV_CH), F32) + dwb_ref[...]
        for k in range(CONV_WIDTH):
            shift = (base + k) % SUBLANES
            lo = r0 + base + k - shift
            rows = hbuf[lo:lo + rc, :] if shift == 0 else hsh[shift - 1, lo:lo + rc, :]
            acc = acc + dww_ref[k:k + 1, :] * rows
        mu = jnp.mean(acc, axis=-1, keepdims=True)
        d = acc - mu
        var = jnp.mean(d * d, axis=-1, keepdims=True)
        hn = d * lax.rsqrt(var + 1e-5) * lnw_ref[...] + lnb_ref[...]
        act[r0:r0 + rc, :] = _silu(hn).astype(BF16)

    y = _dot(act[...], pw_ref[...]) * _silu(cgate_ref[0])
    y_ref[0] = y.astype(BF16)


def _conv_branch(u3, dw_w, dw_b, ln_w, ln_b, pw_w):
    b, s, _ = u3.shape
    ts = CONV_TS
    fixed = lambda bi, si: (0, 0)
    return pl.pallas_call(
        _conv_kernel,
        grid=(b, s // ts),
        in_specs=[
            pl.BlockSpec((1, ts, 2 * CONV_CH), lambda bi, si: (bi, si, 0)),
            pl.BlockSpec((1, ts, CONV_CH), lambda bi, si: (bi, si, 2)),
            pl.BlockSpec((32, CONV_CH), fixed),
            pl.BlockSpec((1, CONV_CH), fixed),
            pl.BlockSpec((1, CONV_CH), fixed),
            pl.BlockSpec((1, CONV_CH), fixed),
            pl.BlockSpec((CONV_CH, CONV_CH), fixed),
        ],
        out_specs=pl.BlockSpec((1, ts, CONV_CH), lambda bi, si: (bi, si, 0)),
        out_shape=jax.ShapeDtypeStruct((b, s, CONV_CH), BF16),
        scratch_shapes=[pltpu.VMEM((ts + CONV_HALO, CONV_CH), F32),
                        pltpu.VMEM((SUBLANES - 1, ts + CONV_HALO, CONV_CH), F32),
                        pltpu.VMEM((ts, CONV_CH), BF16)],
        compiler_params=_cparams(("arbitrary", "arbitrary")),
        name="conv_branch",
    )(u3, u3, dw_w, dw_b, ln_w, ln_b, pw_w)


def _split3(x):
    hi = x.astype(BF16)
    r1 = x - hi.astype(F32)
    mid = r1.astype(BF16)
    lo = (r1 - mid.astype(F32)).astype(BF16)
    return hi, mid, lo


def _blockdiag(p, low):
    z = jnp.zeros_like(p)
    return jnp.concatenate([jnp.where(low, p, z), jnp.where(low, z, p)], axis=0)


def _unit_lower_inverse_pairs(a_list, ii, jj, low):
    c = a_list[0].shape[0]
    same = lambda log2: jnp.right_shift(ii, log2) == jnp.right_shift(jj, log2)
    in_blk = same(1)
    rs = [-jnp.where(in_blk, a, 0.0) for a in a_list]
    lg = 1
    while (1 << lg) < c:
        in_pair = same(lg + 1)
        es = [jnp.where(in_pair, jnp.where(in_blk, 0.0, a), 0.0) for a in a_list]
        rbs = [r.astype(BF16) for r in rs]
        xs = [_dot(rb, _blockdiag(e.astype(BF16), low)) for rb, e in zip(rbs, es)]
        ys = [_dot((e + x).astype(BF16), _blockdiag(rb, low)) for e, x, rb in zip(es, xs, rbs)]
        rs = [r - e - x - y for r, e, x, y in zip(rs, es, xs, ys)]
        in_blk = in_pair
        lg += 1
    return rs


def _gdn_kernel(q_ref, k_ref, v_ref, z_ref, ba_ref, cw_ref, gp_ref, nw_ref, tril_ref, y_ref,
                carry, xbuf, qn, kn, vn, gcum, beta, tbuf, qkbuf, kdbuf, obuf, state):
    nb, ts, c = GDN_NB, GDN_TS, GDN_CHUNK
    n_chunks = ts // c
    w3 = 3 * GDN_W
    heads = range(GDN_HEADS)
    hsl = lambda h: slice(h * GDN_DK, (h + 1) * GDN_DK)
    s = pl.program_id(1)

    @pl.when(s == 0)
    def _():
        carry[...] = jnp.zeros(carry.shape, F32)
        state[...] = jnp.zeros(state.shape, F32)

    base = SUBLANES - (SHORT_CONV - 1)
    gsteps = []
    for bb in range(nb):
        rows_b = slice(bb * ts, (bb + 1) * ts)
        xbuf[bb, 0:SUBLANES, :] = carry[bb]
        xbuf[bb, SUBLANES:SUBLANES + ts, 0:GDN_W] = q_ref[bb]
        xbuf[bb, SUBLANES:SUBLANES + ts, GDN_W:2 * GDN_W] = k_ref[bb]
        xbuf[bb, SUBLANES:SUBLANES + ts, 2 * GDN_W:w3] = v_ref[bb]
        carry[bb] = xbuf[bb, ts:ts + SUBLANES, :]
        for part, dst in ((0, qn), (1, kn), (2, vn)):
            for h in heads:
                lo = part * GDN_W + h * GDN_DK
                acc = jnp.zeros((ts, GDN_DK), F32)
                for t in range(SHORT_CONV):
                    acc = acc + cw_ref[t:t + 1, lo:lo + GDN_DK] * xbuf[bb, base + t:base + t + ts, lo:lo + GDN_DK]
                acc = _silu(acc)
                if part == 0:
                    acc = (acc * lax.rsqrt(jnp.sum(acc * acc, axis=-1, keepdims=True) + 1e-6)) * (GDN_DK ** -0.5)
                elif part == 1:
                    acc = acc * lax.rsqrt(jnp.sum(acc * acc, axis=-1, keepdims=True) + 1e-6)
                dst[rows_b, hsl(h)] = acc

        ba = ba_ref[bb]
        beta[rows_b, :] = _sigmoid(ba[:, 0:LANES])
        xa = ba[:, LANES:2 * LANES] + gp_ref[1:2, :]
        softplus = jnp.maximum(xa, 0.0) + jnp.log(1.0 + jnp.exp(-jnp.abs(xa)))
        gsteps.append(-jnp.exp(gp_ref[0:1, :]) * softplus)

    tril = tril_ref[...]
    g_hi, g_mid, g_lo = _split3(jnp.concatenate(gsteps, axis=0))
    gcum[...] = _dot(tril, g_hi) + _dot(tril, g_mid) + _dot(tril, g_lo)

    ii = lax.broadcasted_iota(jnp.int32, (c, 2 * c), 0)
    lane = lax.broadcasted_iota(jnp.int32, (c, 2 * c), 1)
    jj = jnp.bitwise_and(lane, c - 1)
    low = lane < c
    low_row = low[0:1, :]
    causal = ii >= jj
    strict = ii > jj
    eye = jnp.where(ii == jj, 1.0, 0.0)
    pairs = range(GDN_HEADS // 2)

    def t_body(gi, _):
        items = []
        for cc in range(GDN_TGROUP):
            unit = gi * GDN_TGROUP + cc
            rows_u = pl.ds(pl.multiple_of(unit * c, c), c)
            g_blk = gcum[rows_u, :]
            g_t = g_blk.T
            g_t2 = jnp.concatenate([g_t, g_t], axis=1)
            b_blk = beta[rows_u, :]
            for h in heads:
                gr = g_t[h:h + 1, :]
                kdbuf[unit * GDN_HEADS + h] = (kn[rows_u, hsl(h)].T * jnp.exp(gr[:, c - 1:c] - gr)).astype(BF16)
            for p in pairs:
                ha, hb = 2 * p, 2 * p + 1
                gc = jnp.where(low, g_blk[:, ha:ha + 1], g_blk[:, hb:hb + 1])
                gr = jnp.where(low_row, g_t2[ha:ha + 1, :], g_t2[hb:hb + 1, :])
                bc = jnp.where(low, b_blk[:, ha:ha + 1], b_blk[:, hb:hb + 1])
                items.append((unit * (GDN_HEADS // 2) + p, rows_u, ha, hb, gc - gr, bc))
        zero = jnp.zeros((c, GDN_DK), BF16)
        decays, kks, qks = [], [], []
        for _, rows_u, ha, hb, gdiff, _ in items:
            ka, kb = kn[rows_u, hsl(ha)].astype(BF16), kn[rows_u, hsl(hb)].astype(BF16)
            qa, qb = qn[rows_u, hsl(ha)].astype(BF16), qn[rows_u, hsl(hb)].astype(BF16)
            k_diag = jnp.concatenate([jnp.concatenate([ka, zero], axis=1), jnp.concatenate([zero, kb], axis=1)], axis=0)
            decays.append(jnp.where(causal, jnp.exp(jnp.where(causal, gdiff, 0.0)), 0.0))
            kks.append(_dot_nt(jnp.concatenate([ka, kb], axis=1), k_diag))
            qks.append(_dot_nt(jnp.concatenate([qa, qb], axis=1), k_diag))
        a_s = [jnp.where(strict, it[5] * kk * dc, 0.0) for it, kk, dc in zip(items, kks, decays)]
        for it, qk, dc in zip(items, qks, decays):
            qkbuf[it[0]] = _blockdiag(jnp.where(causal, qk * dc, 0.0).astype(BF16), low)
        rs = _unit_lower_inverse_pairs(a_s, ii, jj, low)
        for it, r in zip(items, rs):
            tbuf[it[0]] = _blockdiag((r + eye).astype(BF16), low)
        return 0

    lax.fori_loop(0, nb * n_chunks // GDN_TGROUP, t_body, 0)

    chains = [(bb, h) for bb in range(nb) for h in heads]

    def chunk_body(ci_, _):
        rows, mats, pair_mats, gcs, bcs = [], [], [], [], []
        for bb in range(nb):
            unit = bb * n_chunks + ci_
            rows_u = pl.ds(pl.multiple_of(unit * c, c), c)
            g_blk = gcum[rows_u, :]
            b_blk = beta[rows_u, :]
            pair_mats += [unit * (GDN_HEADS // 2) + p for p in pairs]
            for h in heads:
                rows.append(rows_u)
                mats.append(unit * GDN_HEADS + h)
                gcs.append(g_blk[:, h:h + 1])
                bcs.append(b_blk[:, h:h + 1])
        n = range(len(chains))
        egs = [jnp.exp(gc) for gc in gcs]
        glasts = [gc[c - 1:c, :] for gc in gcs]
        s_hs = [state[bb * GDN_HEADS + h] for bb, h in chains]
        s_bs = [sh.astype(BF16) for sh in s_hs]
        kss = [_dot(kn[rows[i], hsl(chains[i][1])].astype(BF16), s_bs[i]) for i in n]
        o_state = [_dot((qn[rows[i], hsl(chains[i][1])] * egs[i]).astype(BF16), s_bs[i]) for i in n]
        us = [bcs[i] * (vn[rows[i], hsl(chains[i][1])] - egs[i] * kss[i]) for i in n]
        half = range(len(chains) // 2)
        v_pairs = [_dot(tbuf[pair_mats[j]],
                        jnp.concatenate([us[2 * j], us[2 * j + 1]], axis=0).astype(BF16)).astype(BF16) for j in half]
        vbs = [v_pairs[i // 2][(i % 2) * c:(i % 2 + 1) * c, :] for i in n]
        for i, (bb, h) in enumerate(chains):
            state[bb * GDN_HEADS + h] = s_hs[i] * jnp.exp(glasts[i]) + _dot(kdbuf[mats[i]], vbs[i])
        o_pairs = [_dot(qkbuf[pair_mats[j]], v_pairs[j]) for j in half]
        for i, (bb, h) in enumerate(chains):
            obuf[rows[i], hsl(h)] = o_state[i] + o_pairs[i // 2][(i % 2) * c:(i % 2 + 1) * c, :]
        return 0

    lax.fori_loop(0, n_chunks, chunk_body, 0)

    for bb in range(nb):
        for h in heads:
            o = obuf[bb * ts:(bb + 1) * ts, hsl(h)]
            on = (o * lax.rsqrt(jnp.mean(o * o, axis=-1, keepdims=True) + 1e-6)) * nw_ref[...]
            y_ref[bb, :, hsl(h)] = (on * _silu(z_ref[bb, :, hsl(h)])).astype(BF16)


def _gdn_branch(u3, ba3, conv_w, gparams, norm_w):
    b, s, _ = u3.shape
    nb, ts, c = GDN_NB, GDN_TS, GDN_CHUNK
    assert b % nb == 0 and (nb * ts // c) % GDN_TGROUP == 0
    fixed = lambda bi, si: (0, 0)
    col = lambda j: (lambda bi, si: (bi, si, j))
    rows = nb * ts
    ri = lax.broadcasted_iota(jnp.int32, (rows, rows), 0)
    ci = lax.broadcasted_iota(jnp.int32, (rows, rows), 1)
    tril = ((ri // c == ci // c) & (ri >= ci)).astype(BF16)
    n_mats = (rows // c) * GDN_HEADS
    return pl.pallas_call(
        _gdn_kernel,
        grid=(b // nb, s // ts),
        in_specs=[
            pl.BlockSpec((nb, ts, GDN_W), col(COL_GQ)),
            pl.BlockSpec((nb, ts, GDN_W), col(COL_GK)),
            pl.BlockSpec((nb, ts, GDN_W), col(COL_GV)),
            pl.BlockSpec((nb, ts, GDN_W), col(COL_GZ)),
            pl.BlockSpec((nb, ts, BA_W), col(0)),
            pl.BlockSpec((SUBLANES, 3 * GDN_W), fixed),
            pl.BlockSpec((SUBLANES, LANES), fixed),
            pl.BlockSpec((1, GDN_DV), fixed),
            pl.BlockSpec((rows, rows), fixed),
        ],
        out_specs=pl.BlockSpec((nb, ts, GDN_W), col(0)),
        out_shape=jax.ShapeDtypeStruct((b, s, GDN_W), BF16),
        scratch_shapes=[
            pltpu.VMEM((nb, SUBLANES, 3 * GDN_W), F32),
            pltpu.VMEM((nb, ts + SUBLANES, 3 * GDN_W), F32),
            pltpu.VMEM((rows, GDN_W), F32),
            pltpu.VMEM((rows, GDN_W), F32),
            pltpu.VMEM((rows, GDN_W), F32),
            pltpu.VMEM((rows, LANES), F32),
            pltpu.VMEM((rows, LANES), F32),
            pltpu.VMEM((n_mats // 2, 2 * c, 2 * c), BF16),
            pltpu.VMEM((n_mats // 2, 2 * c, 2 * c), BF16),
            pltpu.VMEM((n_mats, GDN_DK, c), BF16),
            pltpu.VMEM((rows, GDN_W), F32),
            pltpu.VMEM((nb * GDN_HEADS, GDN_DK, GDN_DV), F32),
        ],
        compiler_params=_cparams(("arbitrary", "arbitrary")),
        name="gdn_branch",
    )(u3, u3, u3, u3, ba3, conv_w, gparams, norm_w, tril)


def _strided_rows(r, n, dil):
    return slice(None) if dil == 1 else pl.ds(r, n, stride=dil)


def _attprep_kernel(q_ref, k_ref, v_ref, c_ref, s1_ref, s2_ref, *rest):
    outs, (qs, ks, vs) = rest[:-3], rest[-3:]
    ts = PREP_TS
    cs, s1, s2 = c_ref[...], s1_ref[...], s2_ref[...]
    for p in range(ATT_W // LANES):
        sl = slice(p * LANES, (p + 1) * LANES)
        for src, dst, scale in ((q_ref, qs, LOG2E * ATT_HD ** -0.5), (k_ref, ks, None)):
            x = src[0, :, sl]
            r = x * cs + pltpu.roll(x, LANES - ROPE_DIM // 2, axis=1) * s1 + pltpu.roll(x, ROPE_DIM // 2, axis=1) * s2
            dst[p] = r if scale is None else r * scale
        vs[p] = v_ref[0, :, sl]
    for di, (_, dil) in enumerate(DIL_PATTERNS):
        n = ts // dil
        for r in range(dil):
            rows = _strided_rows(r, n, dil)
            for p in range(ATT_W // LANES):
                sl = slice(p * LANES, (p + 1) * LANES)
                for dst, src in zip(outs[3 * di:3 * di + 3], (qs, ks, vs)):
                    dst[0, r, :, sl] = src[p, rows, :].astype(BF16)


def _att_prep(u3, cos_t, sin_a, sin_b):
    b, s, _ = u3.shape
    ts = PREP_TS
    col = lambda j: (lambda bi, si: (bi, si, j))
    tab = pl.BlockSpec((ts, LANES), lambda bi, si: (si, 0))
    out_specs, out_shape = [], []
    for _, dil in DIL_PATTERNS:
        out_specs += [pl.BlockSpec((1, dil, ts // dil, ATT_W), lambda bi, si: (bi, 0, si, 0))] * 3
        out_shape += [jax.ShapeDtypeStruct((b, dil, s // dil, ATT_W), BF16)] * 3
    return pl.pallas_call(
        _attprep_kernel,
        grid=(b, s // ts),
        in_specs=[pl.BlockSpec((1, ts, ATT_W), col(COL_AQ)), pl.BlockSpec((1, ts, ATT_W), col(COL_AK)),
                  pl.BlockSpec((1, ts, ATT_W), col(COL_AV)), tab, tab, tab],
        out_specs=out_specs,
        out_shape=out_shape,
        scratch_shapes=[pltpu.VMEM((ATT_W // LANES, ts, LANES), F32)] * 3,
        compiler_params=_cparams(("arbitrary", "arbitrary")),
        name="att_prep",
    )(u3, u3, u3, cos_t, sin_a, sin_b)


def _att_kernel(q_ref, k_ref, v_ref, o_ref, lse_ref, kprev, vprev):
    blk = ATT_BLOCK
    n = pl.program_id(2)

    @pl.when(n == 0)
    def _():
        kprev[...] = jnp.zeros(kprev.shape, BF16)
        vprev[...] = jnp.zeros(vprev.shape, BF16)

    row = lax.broadcasted_iota(jnp.int32, (blk, 2 * blk), 0)
    col = lax.broadcasted_iota(jnp.int32, (blk, 2 * blk), 1)
    first = jnp.where(n > 0, 0, 4 * blk)
    ahead = col - row
    lane = lax.broadcasted_iota(jnp.int32, (blk, LANES), 1)
    low = lane < ATT_HD

    for j in range(ATT_STEP_BLOCKS):
        rows = slice(j * blk, (j + 1) * blk)
        if j == 0:
            k_before, v_before = kprev, vprev
            valid = jnp.where(col < blk, ahead - first, blk - ahead) >= 0
        else:
            k_before, v_before = k_ref.at[(j - 1) * blk:j * blk], v_ref.at[(j - 1) * blk:j * blk]
            valid = jnp.where(col < blk, ahead, blk - ahead) >= 0

        scores = []
        for p in range(ATT_W // LANES):
            sl = slice(p * LANES, (p + 1) * LANES)
            kcat = jnp.concatenate([k_before[:, sl], k_ref[rows, sl]], axis=0)
            qp = q_ref[rows, sl]
            for hh in range(2):
                hm = low if hh == 0 else jnp.logical_not(low)
                scores.append(_dot_nt(jnp.where(hm, qp, jnp.zeros_like(qp)), kcat))

        lse_tile = jnp.zeros((blk, LANES), F32)
        for p in range(ATT_W // LANES):
            sl = slice(p * LANES, (p + 1) * LANES)
            vcat = jnp.concatenate([v_before[:, sl], v_ref[rows, sl]], axis=0)
            o_pair = None
            for hh in range(2):
                sc = jnp.where(valid, scores[2 * p + hh], NEG_INF)
                m = jnp.max(sc, axis=-1, keepdims=True)
                pe = jnp.exp2(sc - m)
                l = jnp.sum(pe, axis=-1, keepdims=True)
                o = _dot(pe.astype(BF16), vcat) * (1.0 / l)
                o_pair = o if hh == 0 else jnp.where(low, o_pair, o)
                lse_tile = jnp.where(lane == 2 * p + hh, m * LN2 + jnp.log(l), lse_tile)
            o_ref[rows, sl] = o_pair
        lse_ref[rows, :] = lse_tile

    last = slice((ATT_STEP_BLOCKS - 1) * blk, ATT_STEP_BLOCKS * blk)
    kprev[...] = k_ref[last, :]
    vprev[...] = v_ref[last, :]


def _att_pattern(qd, kd, vd):
    b, dil, lr, _ = qd.shape
    blk = ATT_STEP_BLOCKS * ATT_BLOCK
    idx = lambda bi, r, n: (bi, r, n, 0)
    spec = pl.BlockSpec((None, None, blk, ATT_W), idx)
    return pl.pallas_call(
        _att_kernel,
        grid=(b, dil, lr // blk),
        in_specs=[spec, spec, spec],
        out_specs=[spec, pl.BlockSpec((None, None, blk, LANES), idx)],
        out_shape=[jax.ShapeDtypeStruct((b, dil, lr, ATT_W), F32),
                   jax.ShapeDtypeStruct((b, dil, lr, LANES), F32)],
        scratch_shapes=[pltpu.VMEM((ATT_BLOCK, ATT_W), BF16), pltpu.VMEM((ATT_BLOCK, ATT_W), BF16)],
        compiler_params=_cparams(("arbitrary", "arbitrary", "arbitrary")),
        name=f"att_dil{dil}",
    )(qd, kd, vd)


def _combine_patterns(o_refs, l_refs, g_ref, e_ref, ya, oscr, lscr):
    tm = OUTPROJ_TM
    for g, (o_ref, l_ref) in enumerate(zip(o_refs[1:], l_refs[1:])):
        dil = DIL_PATTERNS[g + 1][1]
        n = tm // dil
        for r in range(dil):
            rows = pl.ds(r, n, stride=dil)
            lscr[g, rows, :] = l_ref[r]
            for p in range(ATT_W // LANES):
                oscr[g, p, rows, :] = o_ref[r, :, p * LANES:(p + 1) * LANES]
    l1, l2, l3 = l_refs[0][0], lscr[0], lscr[1]
    mx = jnp.maximum(jnp.maximum(l1, l2), l3)
    e1, e2, e3 = jnp.exp(l1 - mx), jnp.exp(l2 - mx), jnp.exp(l3 - mx)
    inv = 1.0 / (e1 + e2 + e3)
    spread = e_ref[...]
    ws = []
    for w in (e1 * inv, e2 * inv, e3 * inv):
        hi = w.astype(BF16)
        lo = (w - hi.astype(F32)).astype(BF16)
        ws.append(_dot(jnp.concatenate([hi, lo], axis=1), spread))
    for p in range(ATT_W // LANES):
        sl = slice(p * LANES, (p + 1) * LANES)
        tiles = (o_refs[0][0, :, sl], oscr[0, p], oscr[1, p])
        acc = None
        for w, o in zip(ws, tiles):
            t = w[:, sl] * o
            acc = t if acc is None else acc + t
        ya[:, sl] = (acc * _silu(g_ref[:, sl])).astype(BF16)


def _outproj_kernel(yc_ref, yg_ref, o1_ref, o2_ref, o3_ref, l1_ref, l2_ref, l3_ref, g_ref, e_ref, x_ref,
                    wc_ref, wg_ref, wa_ref, nw_ref, o_ref, *rest, final):
    ya, oscr, lscr = rest[-3:]
    acc = _dot(yc_ref[...], wc_ref[...]) + _dot(yg_ref[...], wg_ref[...])
    _combine_patterns((o1_ref, o2_ref, o3_ref), (l1_ref, l2_ref, l3_ref), g_ref, e_ref, ya, oscr, lscr)
    xo = x_ref[...] + (acc + _dot(ya[...], wa_ref[...]))
    if final:
        o_ref[...] = _rms_norm(xo, nw_ref[...])
    else:
        o_ref[...] = xo
        rest[0][...] = _rms_norm(xo, nw_ref[...]).astype(BF16)


def _out_proj(y_conv, y_gdn, outs, lses, u2, x2, w_c, w_g, w_a, layer, next_norm_w, final):
    m = x2.shape[0]
    tm = OUTPROJ_TM
    tiles_per_seq = outs[0].shape[2] // tm
    row = lambda i: (i, 0)
    fixed = lambda i: (0, 0)
    of_layer = lambda i: (layer, 0, 0)
    of_seq = lambda i: (i // tiles_per_seq, 0, i % tiles_per_seq, 0)
    ospecs = [pl.BlockSpec((None, dil, tm // dil, ATT_W), of_seq) for _, dil in DIL_PATTERNS]
    lspecs = [pl.BlockSpec((None, dil, tm // dil, LANES), of_seq) for _, dil in DIL_PATTERNS]
    head_of_lane = lax.broadcasted_iota(jnp.int32, (2 * LANES, ATT_W), 1) // ATT_HD
    spread = (head_of_lane == lax.broadcasted_iota(jnp.int32, (2 * LANES, ATT_W), 0) % LANES).astype(BF16)
    out_specs = [pl.BlockSpec((tm, D_MODEL), row)]
    out_shape = [jax.ShapeDtypeStruct((m, D_MODEL), F32)]
    if not final:
        out_specs.append(pl.BlockSpec((tm, D_MODEL), row))
        out_shape.append(jax.ShapeDtypeStruct((m, D_MODEL), BF16))
    return pl.pallas_call(
        functools.partial(_outproj_kernel, final=final),
        grid=(m // tm,),
        in_specs=[
            pl.BlockSpec((tm, CONV_CH), row),
            pl.BlockSpec((tm, GDN_W), row),
            *ospecs, *lspecs,
            pl.BlockSpec((tm, ATT_W), lambda i: (i, COL_AG)),
            pl.BlockSpec((2 * LANES, ATT_W), fixed),
            pl.BlockSpec((tm, D_MODEL), row),
            pl.BlockSpec((None, CONV_CH, D_MODEL), of_layer),
            pl.BlockSpec((None, GDN_W, D_MODEL), of_layer),
            pl.BlockSpec((None, ATT_W, D_MODEL), of_layer),
            pl.BlockSpec((1, D_MODEL), fixed),
        ],
        out_specs=out_specs,
        out_shape=out_shape,
        scratch_shapes=[pltpu.VMEM((tm, ATT_W), BF16),
                        pltpu.VMEM((2, ATT_W // LANES, tm, LANES), F32),
                        pltpu.VMEM((2, tm, LANES), F32)],
        compiler_params=_cparams(("arbitrary",)),
        name="out_proj_final" if final else "out_proj",
    )(y_conv, y_gdn, *outs, *lses, u2, spread, x2, w_c, w_g, w_a, next_norm_w)


def _rope_lane_tables(s):
    half = ROPE_DIM // 2
    inv = ROPE_THETA ** (-jnp.arange(half, dtype=F32) / half)
    ang = jnp.arange(s, dtype=F32)[:, None] * inv[None, :]
    cos, sin = jnp.cos(ang), jnp.sin(ang)
    ones = jnp.ones((s, ATT_HD - ROPE_DIM), F32)
    zeros = jnp.zeros((s, ATT_HD - half), F32)
    cos_h = jnp.concatenate([cos, cos, ones], axis=-1)
    sin_a = jnp.concatenate([-sin, zeros], axis=-1)
    sin_b = jnp.concatenate([jnp.zeros((s, half), F32), sin, zeros[:, half:]], axis=-1)
    rep = lambda t: jnp.concatenate([t, t], axis=-1)
    return rep(cos_h), rep(sin_a), rep(sin_b)


def kernel(x, norm_w, w_in, conv_qkv_w, a_log, dt_bias, gdn_norm_w, conf_dw_w, conf_dw_b, conf_ln_w,
           conf_ln_b, conf_pw_w, w_out, final_norm_w):
    b, s, d = x.shape
    depth = w_in.shape[0]
    m = b * s
    assert d == D_MODEL and m % INPROJ_TM == 0 and s % (DIL_PATTERNS[-1][1] * ATT_BLOCK * ATT_STEP_BLOCKS) == 0

    ba0 = 2 * CONV_CH + CONV_CH + 4 * GDN_W
    w_main = jnp.concatenate([w_in[:, :, :ba0], w_in[:, :, ba0 + 2 * GDN_HEADS:]], axis=-1).astype(BF16)
    w_ba = jnp.zeros((depth, d, BA_W), F32)
    w_ba = w_ba.at[:, :, 0:GDN_HEADS].set(w_in[:, :, ba0:ba0 + GDN_HEADS])
    w_ba = w_ba.at[:, :, LANES:LANES + GDN_HEADS].set(w_in[:, :, ba0 + GDN_HEADS:ba0 + 2 * GDN_HEADS])
    w_ba = w_ba.astype(BF16)
    w_out_b = w_out.astype(BF16)
    w_out_c, w_out_g, w_out_a = (w_out_b[:, :CONV_CH], w_out_b[:, CONV_CH:CONV_CH + GDN_W],
                                 w_out_b[:, CONV_CH + GDN_W:])
    pw_b = conf_pw_w.astype(BF16)
    dw_w = jnp.pad(conf_dw_w, ((0, 0), (0, 32 - CONV_WIDTH), (0, 0)))
    cw = jnp.pad(conv_qkv_w, ((0, 0), (0, SUBLANES - SHORT_CONV), (0, 0)))
    gparams = jnp.zeros((depth, SUBLANES, LANES), F32)
    gparams = gparams.at[:, 0, 0:GDN_HEADS].set(a_log).at[:, 1, 0:GDN_HEADS].set(dt_bias)
    cos_t, sin_a, sin_b = _rope_lane_tables(s)

    x2 = x.reshape(m, d)
    xn = _pre_norm(x2, norm_w[0][None, :])
    for l in range(depth):
        last = l == depth - 1
        u, ba = _in_proj(xn, w_main, w_ba, l)
        u3 = u.reshape(b, s, MAIN_W)
        y_conv = _conv_branch(u3, dw_w[l], conf_dw_b[l][None, :], conf_ln_w[l][None, :], conf_ln_b[l][None, :],
                              pw_b[l])
        y_gdn = _gdn_branch(u3, ba.reshape(b, s, BA_W), cw[l], gparams[l], gdn_norm_w[l][None, :])
        qkv = _att_prep(u3, cos_t, sin_a, sin_b)
        outs, lses = [], []
        for g in range(len(DIL_PATTERNS)):
            o_g, lse_g = _att_pattern(*qkv[3 * g:3 * g + 3])
            outs.append(o_g)
            lses.append(lse_g)
        next_w = final_norm_w if last else norm_w[l + 1]
        res = _out_proj(y_conv.reshape(m, CONV_CH), y_gdn.reshape(m, GDN_W), outs, lses, u, x2,
                        w_out_c, w_out_g, w_out_a, l, next_w[None, :], final=last)
        x2 = res[0]
        xn = None if last else res[1]
    return x2.reshape(b, s, d)
```

```python
import functools

import jax
import jax.numpy as jnp
from jax import lax
from jax.experimental import pallas as pl
from jax.experimental.pallas import tpu as pltpu

F32 = jnp.float32
BF16 = jnp.bfloat16

D_MODEL = 2048
GDN_DK = 128
GDN_DV = 128
GDN_W = 768
GDN_HEADS = 6
ATT_HD = 64
ATT_W = 768
ATT_HEADS = 12
CONV_CH = 512
CONV_WIDTH = 31
SHORT_CONV = 4
GDN_CHUNK = 64
ROPE_THETA = 500000.0
ROPE_DIM = 16
DIL_PATTERNS = ((128, 1), (512, 4), (2048, 16))
ATT_BLOCK = 128
NEG_INF = -1e30
LOG2E = 1.4426950408889634
LN2 = 0.6931471805599453
MAIN_W = 2 * CONV_CH + CONV_CH + 4 * GDN_W + 4 * ATT_W
BA_W = 256

COL_GQ, COL_GK, COL_GV, COL_GZ, COL_AQ, COL_AK, COL_AV, COL_AG = 2, 3, 4, 5, 6, 7, 8, 9

LANES = 128
SUBLANES = 8
VMEM_LIMIT = 60 * 1024 * 1024
INPROJ_TM, INPROJ_TN = 2048, 768
INPROJ_LO_TILES = (2 * CONV_CH + CONV_CH + 4 * GDN_W) // INPROJ_TN
OUTPROJ_TM = 512
CONV_TS, CONV_RC, CONV_HALO = 256, 32, 32
GDN_NB, GDN_TS = 4, 128
GDN_TGROUP = 4
ATT_STEP_BLOCKS = 4
PREP_TS = 512


def _cparams(sem):
    return pltpu.CompilerParams(dimension_semantics=sem, vmem_limit_bytes=VMEM_LIMIT)


def _sigmoid(x):
    return 1.0 / (1.0 + jnp.exp(-x))


def _silu(x):
    return x * _sigmoid(x)


def _dot(a, b):
    return jnp.dot(a, b, preferred_element_type=F32)


def _dot_nt(a, b):
    return lax.dot_general(a, b, (((1,), (1,)), ((), ())), preferred_element_type=F32)


def _rms_norm(x, w):
    ms = jnp.mean(x * x, axis=-1, keepdims=True)
    return (x * lax.rsqrt(ms + 1e-6)) * w


def _prenorm_kernel(x_ref, nw_ref, xn_ref):
    xn_ref[...] = _rms_norm(x_ref[...], nw_ref[...]).astype(BF16)


def _pre_norm(x2, norm_w):
    m = x2.shape[0]
    tm = OUTPROJ_TM
    return pl.pallas_call(
        _prenorm_kernel,
        grid=(m // tm,),
        in_specs=[pl.BlockSpec((tm, D_MODEL), lambda i: (i, 0)), pl.BlockSpec((1, D_MODEL), lambda i: (0, 0))],
        out_specs=pl.BlockSpec((tm, D_MODEL), lambda i: (i, 0)),
        out_shape=jax.ShapeDtypeStruct((m, D_MODEL), BF16),
        compiler_params=_cparams(("arbitrary",)),
        name="pre_norm",
    )(x2, norm_w)


def _inproj_kernel(xn_ref, wlo_ref, whi_ref, wba_ref, u_ref, ba_ref):
    j = pl.program_id(1)

    @pl.when(j == 0)
    def _():
        ba_ref[...] = _dot(xn_ref[...], wba_ref[...])

    @pl.when(j < INPROJ_LO_TILES)
    def _():
        u_ref[...] = _dot(xn_ref[...], wlo_ref[...])

    @pl.when(j >= INPROJ_LO_TILES)
    def _():
        u_ref[...] = _dot(xn_ref[...], whi_ref[...])


def _in_proj(xn, w_lo, w_hi, w_ba, layer):
    m = xn.shape[0]
    tm, tn = INPROJ_TM, INPROJ_TN
    n_lo = INPROJ_LO_TILES
    return pl.pallas_call(
        _inproj_kernel,
        grid=(m // tm, MAIN_W // tn),
        in_specs=[
            pl.BlockSpec((tm, D_MODEL), lambda i, j: (i, 0)),
            pl.BlockSpec((None, D_MODEL, tn), lambda i, j: (layer, 0, jnp.minimum(j, n_lo - 1))),
            pl.BlockSpec((None, D_MODEL, tn), lambda i, j: (layer, 0, jnp.maximum(j - n_lo, 0))),
            pl.BlockSpec((None, D_MODEL, BA_W), lambda i, j: (layer, 0, 0)),
        ],
        out_specs=[
            pl.BlockSpec((tm, tn), lambda i, j: (i, j)),
            pl.BlockSpec((tm, BA_W), lambda i, j: (i, 0)),
        ],
        out_shape=[jax.ShapeDtypeStruct((m, MAIN_W), F32), jax.ShapeDtypeStruct((m, BA_W), F32)],
        compiler_params=_cparams(("arbitrary", "arbitrary")),
        name="in_proj",
    )(xn, w_lo, w_hi, w_ba)


def _conv_kernel(cin_ref, cgate_ref, dww_ref, dwb_ref, lnw_ref, lnb_ref, pw_ref, y_ref, hbuf, hsh, act):
    ts, rc, halo = CONV_TS, CONV_RC, CONV_HALO
    s = pl.program_id(1)

    @pl.when(s == 0)
    def _():
        hbuf[0:halo, :] = jnp.zeros((halo, CONV_CH), F32)

    @pl.when(s > 0)
    def _():
        hbuf[0:halo, :] = hbuf[ts:ts + halo, :]

    u = cin_ref[0]
    hbuf[halo:halo + ts, :] = u[:, :CONV_CH] * _sigmoid(u[:, CONV_CH:])

    span = ts + halo - SUBLANES
    for r in range(1, SUBLANES):
        hsh[r - 1, 0:span, :] = hbuf[r:r + span, :]

    base = halo - (CONV_WIDTH - 1)

    for c in range(ts // rc):
        r0 = c * rc
        acc = jnp.zeros((rc, CONV_CH), F32) + dwb_ref[...]
        for k in range(CONV_WIDTH):
            shift = (base + k) % SUBLANES
            lo = r0 + base + k - shift
            rows = hbuf[lo:lo + rc, :] if shift == 0 else hsh[shift - 1, lo:lo + rc, :]
            acc = acc + dww_ref[k:k + 1, :] * rows
        mu = jnp.mean(acc, axis=-1, keepdims=True)
        d = acc - mu
        var = jnp.mean(d * d, axis=-1, keepdims=True)
        hn = d * lax.rsqrt(var + 1e-5) * lnw_ref[...] + lnb_ref[...]
        act[r0:r0 + rc, :] = _silu(hn).astype(BF16)

    y = _dot(act[...], pw_ref[...]) * _silu(cgate_ref[0])
    y_ref[0] = y.astype(BF16)


def _conv_branch(u3, dw_w, dw_b, ln_w, ln_b, pw_w):
    b, s, _ = u3.shape
    ts = CONV_TS
    fixed = lambda bi, si: (0, 0)
    return pl.pallas_call(
        _conv_kernel,
        grid=(b, s // ts),
        in_specs=[
            pl.BlockSpec((1, ts, 2 * CONV_CH), lambda bi, si: (bi, si, 0)),
            pl.BlockSpec((1, ts, CONV_CH), lambda bi, si: (bi, si, 2)),
            pl.BlockSpec((32, CONV_CH), fixed),
            pl.BlockSpec((1, CONV_CH), fixed),
            pl.BlockSpec((1, CONV_CH), fixed),
            pl.BlockSpec((1, CONV_CH), fixed),
            pl.BlockSpec((CONV_CH, CONV_CH), fixed),
        ],
        out_specs=pl.BlockSpec((1, ts, CONV_CH), lambda bi, si: (bi, si, 0)),
        out_shape=jax.ShapeDtypeStruct((b, s, CONV_CH), BF16),
        scratch_shapes=[pltpu.VMEM((ts + CONV_HALO, CONV_CH), F32),
                        pltpu.VMEM((SUBLANES - 1, ts + CONV_HALO, CONV_CH), F32),
                        pltpu.VMEM((ts, CONV_CH), BF16)],
        compiler_params=_cparams(("arbitrary", "arbitrary")),
        name="conv_branch",
    )(u3, u3, dw_w, dw_b, ln_w, ln_b, pw_w)


def _split3(x):
    hi = x.astype(BF16)
    r1 = x - hi.astype(F32)
    mid = r1.astype(BF16)
    lo = (r1 - mid.astype(F32)).astype(BF16)
    return hi, mid, lo


def _blockdiag(p, low):
    z = jnp.zeros_like(p)
    return jnp.concatenate([jnp.where(low, p, z), jnp.where(low, z, p)], axis=0)


def _unit_lower_inverse_pairs(a_list, ii, jj, low):
    c = a_list[0].shape[0]
    same = lambda log2: jnp.right_shift(ii, log2) == jnp.right_shift(jj, log2)
    in_blk = same(1)
    rs = [-jnp.where(in_blk, a, 0.0) for a in a_list]
    lg = 1
    while (1 << lg) < c:
        in_pair = same(lg + 1)
        es = [jnp.where(in_pair, jnp.where(in_blk, 0.0, a), 0.0) for a in a_list]
        rbs = [r.astype(BF16) for r in rs]
        xs = [_dot(rb, _blockdiag(e.astype(BF16), low)) for rb, e in zip(rbs, es)]
        ys = [_dot((e + x).astype(BF16), _blockdiag(rb, low)) for e, x, rb in zip(es, xs, rbs)]
        rs = [r - e - x - y for r, e, x, y in zip(rs, es, xs, ys)]
        in_blk = in_pair
        lg += 1
    return rs


def _gdn_kernel(q_ref, k_ref, v_ref, z_ref, ba_ref, cw_ref, gp_ref, nw_ref, tril_ref, y_ref,
                carry, xbuf, qn, kn, vn, gcum, beta, tbuf, qkbuf, kdbuf, obuf, state):
    nb, ts, c = GDN_NB, GDN_TS, GDN_CHUNK
    n_chunks = ts // c
    w3 = 3 * GDN_W
    heads = range(GDN_HEADS)
    hsl = lambda h: slice(h * GDN_DK, (h + 1) * GDN_DK)
    s = pl.program_id(1)

    @pl.when(s == 0)
    def _():
        carry[...] = jnp.zeros(carry.shape, F32)
        state[...] = jnp.zeros(state.shape, F32)

    base = SUBLANES - (SHORT_CONV - 1)
    gsteps = []
    for bb in range(nb):
        rows_b = slice(bb * ts, (bb + 1) * ts)
        xbuf[bb, 0:SUBLANES, :] = carry[bb]
        xbuf[bb, SUBLANES:SUBLANES + ts, 0:GDN_W] = q_ref[bb]
        xbuf[bb, SUBLANES:SUBLANES + ts, GDN_W:2 * GDN_W] = k_ref[bb]
        xbuf[bb, SUBLANES:SUBLANES + ts, 2 * GDN_W:w3] = v_ref[bb]
        carry[bb] = xbuf[bb, ts:ts + SUBLANES, :]
        for part, dst in ((0, qn), (1, kn), (2, vn)):
            for h in heads:
                lo = part * GDN_W + h * GDN_DK
                acc = jnp.zeros((ts, GDN_DK), F32)
                for t in range(SHORT_CONV):
                    acc = acc + cw_ref[t:t + 1, lo:lo + GDN_DK] * xbuf[bb, base + t:base + t + ts, lo:lo + GDN_DK]
                acc = _silu(acc)
                if part == 0:
                    acc = (acc * lax.rsqrt(jnp.sum(acc * acc, axis=-1, keepdims=True) + 1e-6)) * (GDN_DK ** -0.5)
                elif part == 1:
                    acc = acc * lax.rsqrt(jnp.sum(acc * acc, axis=-1, keepdims=True) + 1e-6)
                dst[rows_b, hsl(h)] = acc

        ba = ba_ref[bb]
        beta[rows_b, :] = _sigmoid(ba[:, 0:LANES])
        xa = ba[:, LANES:2 * LANES] + gp_ref[1:2, :]
        softplus = jnp.maximum(xa, 0.0) + jnp.log(1.0 + jnp.exp(-jnp.abs(xa)))
        gsteps.append(-jnp.exp(gp_ref[0:1, :]) * softplus)

    tril = tril_ref[...]
    g_hi, g_mid, g_lo = _split3(jnp.concatenate(gsteps, axis=0))
    gcum[...] = _dot(tril, g_hi) + _dot(tril, g_mid) + _dot(tril, g_lo)

    ii = lax.broadcasted_iota(jnp.int32, (c, 2 * c), 0)
    lane = lax.broadcasted_iota(jnp.int32, (c, 2 * c), 1)
    jj = jnp.bitwise_and(lane, c - 1)
    low = lane < c
    low_row = low[0:1, :]
    causal = ii >= jj
    strict = ii > jj
    eye = jnp.where(ii == jj, 1.0, 0.0)
    pairs = range(GDN_HEADS // 2)

    def t_body(gi, _):
        items = []
        for cc in range(GDN_TGROUP):
            unit = gi * GDN_TGROUP + cc
            rows_u = pl.ds(pl.multiple_of(unit * c, c), c)
            g_blk = gcum[rows_u, :]
            g_t = g_blk.T
            g_t2 = jnp.concatenate([g_t, g_t], axis=1)
            b_blk = beta[rows_u, :]
            for h in heads:
                gr = g_t[h:h + 1, :]
                kdbuf[unit * GDN_HEADS + h] = (kn[rows_u, hsl(h)].T * jnp.exp(gr[:, c - 1:c] - gr)).astype(BF16)
            for p in pairs:
                ha, hb = 2 * p, 2 * p + 1
                gc = jnp.where(low, g_blk[:, ha:ha + 1], g_blk[:, hb:hb + 1])
                gr = jnp.where(low_row, g_t2[ha:ha + 1, :], g_t2[hb:hb + 1, :])
                bc = jnp.where(low, b_blk[:, ha:ha + 1], b_blk[:, hb:hb + 1])
                items.append((unit * (GDN_HEADS // 2) + p, rows_u, ha, hb, gc - gr, bc))
        zero = jnp.zeros((c, GDN_DK), BF16)
        decays, kks, qks = [], [], []
        for _, rows_u, ha, hb, gdiff, _ in items:
            ka, kb = kn[rows_u, hsl(ha)].astype(BF16), kn[rows_u, hsl(hb)].astype(BF16)
            qa, qb = qn[rows_u, hsl(ha)].astype(BF16), qn[rows_u, hsl(hb)].astype(BF16)
            k_diag = jnp.concatenate([jnp.concatenate([ka, zero], axis=1), jnp.concatenate([zero, kb], axis=1)], axis=0)
            decays.append(jnp.where(causal, jnp.exp(jnp.where(causal, gdiff, 0.0)), 0.0))
            kks.append(_dot_nt(jnp.concatenate([ka, kb], axis=1), k_diag))
            qks.append(_dot_nt(jnp.concatenate([qa, qb], axis=1), k_diag))
        a_s = [jnp.where(strict, it[5] * kk * dc, 0.0) for it, kk, dc in zip(items, kks, decays)]
        for it, qk, dc in zip(items, qks, decays):
            qkbuf[it[0]] = _blockdiag(jnp.where(causal, qk * dc, 0.0).astype(BF16), low)
        rs = _unit_lower_inverse_pairs(a_s, ii, jj, low)
        for it, r in zip(items, rs):
            tbuf[it[0]] = _blockdiag((r + eye).astype(BF16), low)
        return 0

    lax.fori_loop(0, nb * n_chunks // GDN_TGROUP, t_body, 0)

    chains = [(bb, h) for bb in range(nb) for h in heads]

    def chunk_body(ci_, _):
        rows, mats, pair_mats, gcs, bcs = [], [], [], [], []
        for bb in range(nb):
            unit = bb * n_chunks + ci_
            rows_u = pl.ds(pl.multiple_of(unit * c, c), c)
            g_blk = gcum[rows_u, :]
            b_blk = beta[rows_u, :]
            pair_mats += [unit * (GDN_HEADS // 2) + p for p in pairs]
            for h in heads:
                rows.append(rows_u)
                mats.append(unit * GDN_HEADS + h)
                gcs.append(g_blk[:, h:h + 1])
                bcs.append(b_blk[:, h:h + 1])
        n = range(len(chains))
        egs = [jnp.exp(gc) for gc in gcs]
        glasts = [gc[c - 1:c, :] for gc in gcs]
        s_hs = [state[bb * GDN_HEADS + h] for bb, h in chains]
        s_bs = [sh.astype(BF16) for sh in s_hs]
        kq_s = [_dot(jnp.concatenate([kn[rows[i], hsl(chains[i][1])].astype(BF16),
                                      (qn[rows[i], hsl(chains[i][1])] * egs[i]).astype(BF16)], axis=0), s_bs[i]) for i in n]
        o_state = [r[c:2 * c, :] for r in kq_s]
        us = [bcs[i] * (vn[rows[i], hsl(chains[i][1])] - egs[i] * kq_s[i][0:c, :]) for i in n]
        half = range(len(chains) // 2)
        v_pairs = [_dot(tbuf[pair_mats[j]],
                        jnp.concatenate([us[2 * j], us[2 * j + 1]], axis=0).astype(BF16)).astype(BF16) for j in half]
        vbs = [v_pairs[i // 2][(i % 2) * c:(i % 2 + 1) * c, :] for i in n]
        for i, (bb, h) in enumerate(chains):
            state[bb * GDN_HEADS + h] = s_hs[i] * jnp.exp(glasts[i]) + _dot(kdbuf[mats[i]], vbs[i])
        o_pairs = [_dot(qkbuf[pair_mats[j]], v_pairs[j]) for j in half]
        for i, (bb, h) in enumerate(chains):
            obuf[rows[i], hsl(h)] = o_state[i] + o_pairs[i // 2][(i % 2) * c:(i % 2 + 1) * c, :]
        return 0

    lax.fori_loop(0, n_chunks, chunk_body, 0)

    for bb in range(nb):
        for h in heads:
            o = obuf[bb * ts:(bb + 1) * ts, hsl(h)]
            on = (o * lax.rsqrt(jnp.mean(o * o, axis=-1, keepdims=True) + 1e-6)) * nw_ref[...]
            y_ref[bb, :, hsl(h)] = (on * _silu(z_ref[bb, :, hsl(h)])).astype(BF16)


def _gdn_branch(u3, ba3, conv_w, gparams, norm_w):
    b, s, _ = u3.shape
    nb, ts, c = GDN_NB, GDN_TS, GDN_CHUNK
    assert b % nb == 0 and (nb * ts // c) % GDN_TGROUP == 0
    fixed = lambda bi, si: (0, 0)
    col = lambda j: (lambda bi, si: (bi, si, j))
    rows = nb * ts
    ri = lax.broadcasted_iota(jnp.int32, (rows, rows), 0)
    ci = lax.broadcasted_iota(jnp.int32, (rows, rows), 1)
    tril = ((ri // c == ci // c) & (ri >= ci)).astype(BF16)
    n_mats = (rows // c) * GDN_HEADS
    return pl.pallas_call(
        _gdn_kernel,
        grid=(b // nb, s // ts),
        in_specs=[
            pl.BlockSpec((nb, ts, GDN_W), col(COL_GQ)),
            pl.BlockSpec((nb, ts, GDN_W), col(COL_GK)),
            pl.BlockSpec((nb, ts, GDN_W), col(COL_GV)),
            pl.BlockSpec((nb, ts, GDN_W), col(COL_GZ)),
            pl.BlockSpec((nb, ts, BA_W), col(0)),
            pl.BlockSpec((SUBLANES, 3 * GDN_W), fixed),
            pl.BlockSpec((SUBLANES, LANES), fixed),
            pl.BlockSpec((1, GDN_DV), fixed),
            pl.BlockSpec((rows, rows), fixed),
        ],
        out_specs=pl.BlockSpec((nb, ts, GDN_W), col(0)),
        out_shape=jax.ShapeDtypeStruct((b, s, GDN_W), BF16),
        scratch_shapes=[
            pltpu.VMEM((nb, SUBLANES, 3 * GDN_W), F32),
            pltpu.VMEM((nb, ts + SUBLANES, 3 * GDN_W), F32),
            pltpu.VMEM((rows, GDN_W), F32),
            pltpu.VMEM((rows, GDN_W), F32),
            pltpu.VMEM((rows, GDN_W), F32),
            pltpu.VMEM((rows, LANES), F32),
            pltpu.VMEM((rows, LANES), F32),
            pltpu.VMEM((n_mats // 2, 2 * c, 2 * c), BF16),
            pltpu.VMEM((n_mats // 2, 2 * c, 2 * c), BF16),
            pltpu.VMEM((n_mats, GDN_DK, c), BF16),
            pltpu.VMEM((rows, GDN_W), F32),
            pltpu.VMEM((nb * GDN_HEADS, GDN_DK, GDN_DV), F32),
        ],
        compiler_params=_cparams(("arbitrary", "arbitrary")),
        name="gdn_branch",
    )(u3, u3, u3, u3, ba3, conv_w, gparams, norm_w, tril)


def _strided_rows(r, n, dil):
    return slice(None) if dil == 1 else pl.ds(r, n, stride=dil)


def _attprep_kernel(q_ref, k_ref, v_ref, c_ref, s1_ref, s2_ref, *rest):
    outs, (qs, ks, vs) = rest[:-3], rest[-3:]
    ts = PREP_TS
    cs, s1, s2 = c_ref[...], s1_ref[...], s2_ref[...]
    for p in range(ATT_W // LANES):
        sl = slice(p * LANES, (p + 1) * LANES)
        for src, dst, scale in ((q_ref, qs, LOG2E * ATT_HD ** -0.5), (k_ref, ks, None)):
            x = src[0, :, sl]
            r = x * cs + pltpu.roll(x, LANES - ROPE_DIM // 2, axis=1) * s1 + pltpu.roll(x, ROPE_DIM // 2, axis=1) * s2
            dst[p] = r if scale is None else r * scale
        vs[p] = v_ref[0, :, sl]
    for di, (_, dil) in enumerate(DIL_PATTERNS):
        n = ts // dil
        for r in range(dil):
            rows = _strided_rows(r, n, dil)
            for p in range(ATT_W // LANES):
                sl = slice(p * LANES, (p + 1) * LANES)
                for dst, src in zip(outs[3 * di:3 * di + 3], (qs, ks, vs)):
                    dst[0, r, :, sl] = src[p, rows, :].astype(BF16)


def _att_prep(u3, cos_t, sin_a, sin_b):
    b, s, _ = u3.shape
    ts = PREP_TS
    col = lambda j: (lambda bi, si: (bi, si, j))
    tab = pl.BlockSpec((ts, LANES), lambda bi, si: (si, 0))
    out_specs, out_shape = [], []
    for _, dil in DIL_PATTERNS:
        out_specs += [pl.BlockSpec((1, dil, ts // dil, ATT_W), lambda bi, si: (bi, 0, si, 0))] * 3
        out_shape += [jax.ShapeDtypeStruct((b, dil, s // dil, ATT_W), BF16)] * 3
    return pl.pallas_call(
        _attprep_kernel,
        grid=(b, s // ts),
        in_specs=[pl.BlockSpec((1, ts, ATT_W), col(COL_AQ)), pl.BlockSpec((1, ts, ATT_W), col(COL_AK)),
                  pl.BlockSpec((1, ts, ATT_W), col(COL_AV)), tab, tab, tab],
        out_specs=out_specs,
        out_shape=out_shape,
        scratch_shapes=[pltpu.VMEM((ATT_W // LANES, ts, LANES), F32)] * 3,
        compiler_params=_cparams(("arbitrary", "arbitrary")),
        name="att_prep",
    )(u3, u3, u3, cos_t, sin_a, sin_b)


def _att_kernel(q_ref, k_ref, v_ref, o_ref, lse_ref, kprev, vprev):
    blk = ATT_BLOCK
    n = pl.program_id(2)

    @pl.when(n == 0)
    def _():
        kprev[...] = jnp.zeros(kprev.shape, BF16)
        vprev[...] = jnp.zeros(vprev.shape, BF16)

    row = lax.broadcasted_iota(jnp.int32, (blk, 2 * blk), 0)
    col = lax.broadcasted_iota(jnp.int32, (blk, 2 * blk), 1)
    first = jnp.where(n > 0, 0, 4 * blk)
    ahead = col - row
    lane = lax.broadcasted_iota(jnp.int32, (blk, LANES), 1)
    low = lane < ATT_HD

    for j in range(ATT_STEP_BLOCKS):
        rows = slice(j * blk, (j + 1) * blk)
        if j == 0:
            k_before, v_before = kprev, vprev
            valid = jnp.where(col < blk, ahead - first, blk - ahead) >= 0
        else:
            k_before, v_before = k_ref.at[(j - 1) * blk:j * blk], v_ref.at[(j - 1) * blk:j * blk]
            valid = jnp.where(col < blk, ahead, blk - ahead) >= 0

        scores = []
        for p in range(ATT_W // LANES):
            sl = slice(p * LANES, (p + 1) * LANES)
            kcat = jnp.concatenate([k_before[:, sl], k_ref[rows, sl]], axis=0)
            qp = q_ref[rows, sl]
            for hh in range(2):
                hm = low if hh == 0 else jnp.logical_not(low)
                scores.append(_dot_nt(jnp.where(hm, qp, jnp.zeros_like(qp)), kcat))

        lse_tile = jnp.zeros((blk, LANES), F32)
        for p in range(ATT_W // LANES):
            sl = slice(p * LANES, (p + 1) * LANES)
            vcat = jnp.concatenate([v_before[:, sl], v_ref[rows, sl]], axis=0)
            o_pair = None
            for hh in range(2):
                sc = jnp.where(valid, scores[2 * p + hh], NEG_INF)
                m = jnp.max(sc, axis=-1, keepdims=True)
                pe = jnp.exp2(sc - m)
                l = jnp.sum(pe, axis=-1, keepdims=True)
                o = _dot(pe.astype(BF16), vcat) * (1.0 / l)
                o_pair = o if hh == 0 else jnp.where(low, o_pair, o)
                lse_tile = jnp.where(lane == 2 * p + hh, m * LN2 + jnp.log(l), lse_tile)
            o_ref[rows, sl] = o_pair
        lse_ref[rows, :] = lse_tile

    last = slice((ATT_STEP_BLOCKS - 1) * blk, ATT_STEP_BLOCKS * blk)
    kprev[...] = k_ref[last, :]
    vprev[...] = v_ref[last, :]


def _att_pattern(qd, kd, vd):
    b, dil, lr, _ = qd.shape
    blk = ATT_STEP_BLOCKS * ATT_BLOCK
    idx = lambda bi, r, n: (bi, r, n, 0)
    spec = pl.BlockSpec((None, None, blk, ATT_W), idx)
    return pl.pallas_call(
        _att_kernel,
        grid=(b, dil, lr // blk),
        in_specs=[spec, spec, spec],
        out_specs=[spec, pl.BlockSpec((None, None, blk, LANES), idx)],
        out_shape=[jax.ShapeDtypeStruct((b, dil, lr, ATT_W), F32),
                   jax.ShapeDtypeStruct((b, dil, lr, LANES), F32)],
        scratch_shapes=[pltpu.VMEM((ATT_BLOCK, ATT_W), BF16), pltpu.VMEM((ATT_BLOCK, ATT_W), BF16)],
        compiler_params=_cparams(("arbitrary", "arbitrary", "arbitrary")),
        name=f"att_dil{dil}",
    )(qd, kd, vd)


def _combine_patterns(o_refs, l_refs, g_ref, e_ref, ya, oscr, lscr):
    tm = OUTPROJ_TM
    for g, (o_ref, l_ref) in enumerate(zip(o_refs[1:], l_refs[1:])):
        dil = DIL_PATTERNS[g + 1][1]
        n = tm // dil
        for r in range(dil):
            rows = pl.ds(r, n, stride=dil)
            lscr[g, rows, :] = l_ref[r]
            for p in range(ATT_W // LANES):
                oscr[g, p, rows, :] = o_ref[r, :, p * LANES:(p + 1) * LANES]
    l1, l2, l3 = l_refs[0][0], lscr[0], lscr[1]
    mx = jnp.maximum(jnp.maximum(l1, l2), l3)
    e1, e2, e3 = jnp.exp(l1 - mx), jnp.exp(l2 - mx), jnp.exp(l3 - mx)
    inv = 1.0 / (e1 + e2 + e3)
    spread = e_ref[...]
    ws = []
    for w in (e1 * inv, e2 * inv, e3 * inv):
        hi = w.astype(BF16)
        lo = (w - hi.astype(F32)).astype(BF16)
        ws.append(_dot(jnp.concatenate([hi, lo], axis=1), spread))
    for p in range(ATT_W // LANES):
        sl = slice(p * LANES, (p + 1) * LANES)
        tiles = (o_refs[0][0, :, sl], oscr[0, p], oscr[1, p])
        acc = None
        for w, o in zip(ws, tiles):
            t = w[:, sl] * o
            acc = t if acc is None else acc + t
        ya[:, sl] = (acc * _silu(g_ref[:, sl])).astype(BF16)


def _outproj_kernel(yc_ref, yg_ref, o1_ref, o2_ref, o3_ref, l1_ref, l2_ref, l3_ref, g_ref, e_ref, x_ref,
                    wc_ref, wg_ref, wa_ref, nw_ref, o_ref, *rest, final):
    ya, oscr, lscr = rest[-3:]
    acc = _dot(yc_ref[...], wc_ref[...]) + _dot(yg_ref[...], wg_ref[...])
    _combine_patterns((o1_ref, o2_ref, o3_ref), (l1_ref, l2_ref, l3_ref), g_ref, e_ref, ya, oscr, lscr)
    xo = x_ref[...] + (acc + _dot(ya[...], wa_ref[...]))
    if final:
        o_ref[...] = _rms_norm(xo, nw_ref[...])
    else:
        o_ref[...] = xo
        rest[0][...] = _rms_norm(xo, nw_ref[...]).astype(BF16)


def _out_proj(y_conv, y_gdn, outs, lses, u2, x2, w_c, w_g, w_a, layer, next_norm_w, final):
    m = x2.shape[0]
    tm = OUTPROJ_TM
    tiles_per_seq = outs[0].shape[2] // tm
    row = lambda i: (i, 0)
    fixed = lambda i: (0, 0)
    of_layer = lambda i: (layer, 0, 0)
    of_seq = lambda i: (i // tiles_per_seq, 0, i % tiles_per_seq, 0)
    ospecs = [pl.BlockSpec((None, dil, tm // dil, ATT_W), of_seq) for _, dil in DIL_PATTERNS]
    lspecs = [pl.BlockSpec((None, dil, tm // dil, LANES), of_seq) for _, dil in DIL_PATTERNS]
    head_of_lane = lax.broadcasted_iota(jnp.int32, (2 * LANES, ATT_W), 1) // ATT_HD
    spread = (head_of_lane == lax.broadcasted_iota(jnp.int32, (2 * LANES, ATT_W), 0) % LANES).astype(BF16)
    out_specs = [pl.BlockSpec((tm, D_MODEL), row)]
    out_shape = [jax.ShapeDtypeStruct((m, D_MODEL), F32)]
    if not final:
        out_specs.append(pl.BlockSpec((tm, D_MODEL), row))
        out_shape.append(jax.ShapeDtypeStruct((m, D_MODEL), BF16))
    return pl.pallas_call(
        functools.partial(_outproj_kernel, final=final),
        grid=(m // tm,),
        in_specs=[
            pl.BlockSpec((tm, CONV_CH), row),
            pl.BlockSpec((tm, GDN_W), row),
            *ospecs, *lspecs,
            pl.BlockSpec((tm, ATT_W), lambda i: (i, COL_AG)),
            pl.BlockSpec((2 * LANES, ATT_W), fixed),
            pl.BlockSpec((tm, D_MODEL), row),
            pl.BlockSpec((None, CONV_CH, D_MODEL), of_layer),
            pl.BlockSpec((None, GDN_W, D_MODEL), of_layer),
            pl.BlockSpec((None, ATT_W, D_MODEL), of_layer),
            pl.BlockSpec((1, D_MODEL), fixed),
        ],
        out_specs=out_specs,
        out_shape=out_shape,
        scratch_shapes=[pltpu.VMEM((tm, ATT_W), BF16),
                        pltpu.VMEM((2, ATT_W // LANES, tm, LANES), F32),
                        pltpu.VMEM((2, tm, LANES), F32)],
        compiler_params=_cparams(("arbitrary",)),
        name="out_proj_final" if final else "out_proj",
    )(y_conv, y_gdn, *outs, *lses, u2, spread, x2, w_c, w_g, w_a, next_norm_w)


def _rope_lane_tables(s):
    half = ROPE_DIM // 2
    inv = ROPE_THETA ** (-jnp.arange(half, dtype=F32) / half)
    ang = jnp.arange(s, dtype=F32)[:, None] * inv[None, :]
    cos, sin = jnp.cos(ang), jnp.sin(ang)
    ones = jnp.ones((s, ATT_HD - ROPE_DIM), F32)
    zeros = jnp.zeros((s, ATT_HD - half), F32)
    cos_h = jnp.concatenate([cos, cos, ones], axis=-1)
    sin_a = jnp.concatenate([-sin, zeros], axis=-1)
    sin_b = jnp.concatenate([jnp.zeros((s, half), F32), sin, zeros[:, half:]], axis=-1)
    rep = lambda t: jnp.concatenate([t, t], axis=-1)
    return rep(cos_h), rep(sin_a), rep(sin_b)


def kernel(x, norm_w, w_in, conv_qkv_w, a_log, dt_bias, gdn_norm_w, conf_dw_w, conf_dw_b, conf_ln_w,
           conf_ln_b, conf_pw_w, w_out, final_norm_w):
    b, s, d = x.shape
    depth = w_in.shape[0]
    m = b * s
    assert d == D_MODEL and m % INPROJ_TM == 0 and s % (DIL_PATTERNS[-1][1] * ATT_BLOCK * ATT_STEP_BLOCKS) == 0

    ba0 = 2 * CONV_CH + CONV_CH + 4 * GDN_W
    w_lo = w_in[:, :, :ba0].astype(BF16)
    w_hi = w_in[:, :, ba0 + 2 * GDN_HEADS:].astype(BF16)
    w_ba = jnp.zeros((depth, d, BA_W), F32)
    w_ba = w_ba.at[:, :, 0:GDN_HEADS].set(w_in[:, :, ba0:ba0 + GDN_HEADS])
    w_ba = w_ba.at[:, :, LANES:LANES + GDN_HEADS].set(w_in[:, :, ba0 + GDN_HEADS:ba0 + 2 * GDN_HEADS])
    w_ba = w_ba.astype(BF16)
    w_out_b = w_out.astype(BF16)
    w_out_c, w_out_g, w_out_a = (w_out_b[:, :CONV_CH], w_out_b[:, CONV_CH:CONV_CH + GDN_W],
                                 w_out_b[:, CONV_CH + GDN_W:])
    pw_b = conf_pw_w.astype(BF16)
    dw_w = jnp.pad(conf_dw_w, ((0, 0), (0, 32 - CONV_WIDTH), (0, 0)))
    cw = jnp.pad(conv_qkv_w, ((0, 0), (0, SUBLANES - SHORT_CONV), (0, 0)))
    gparams = jnp.zeros((depth, SUBLANES, LANES), F32)
    gparams = gparams.at[:, 0, 0:GDN_HEADS].set(a_log).at[:, 1, 0:GDN_HEADS].set(dt_bias)
    cos_t, sin_a, sin_b = _rope_lane_tables(s)

    x2 = x.reshape(m, d)
    xn = _pre_norm(x2, norm_w[0][None, :])
    for l in range(depth):
        last = l == depth - 1
        u, ba = _in_proj(xn, w_lo, w_hi, w_ba, l)
        u3 = u.reshape(b, s, MAIN_W)
        y_conv = _conv_branch(u3, dw_w[l], conf_dw_b[l][None, :], conf_ln_w[l][None, :], conf_ln_b[l][None, :],
                              pw_b[l])
        y_gdn = _gdn_branch(u3, ba.reshape(b, s, BA_W), cw[l], gparams[l], gdn_norm_w[l][None, :])
        qkv = _att_prep(u3, cos_t, sin_a, sin_b)
        outs, lses = [], []
        for g in range(len(DIL_PATTERNS)):
            o_g, lse_g = _att_pattern(*qkv[3 * g:3 * g + 3])
            outs.append(o_g)
            lses.append(lse_g)
        next_w = final_norm_w if last else norm_w[l + 1]
        res = _out_proj(y_conv.reshape(m, CONV_CH), y_gdn.reshape(m, GDN_W), outs, lses, u, x2,
                        w_out_c, w_out_g, w_out_a, l, next_w[None, :], final=last)
        x2 = res[0]
        xn = None if last else res[1]
    return x2.reshape(b, s, d)
```

```python
import functools

import jax
import jax.numpy as jnp
from jax import lax
from jax.experimental import pallas as pl
from jax.experimental.pallas import tpu as pltpu

F32 = jnp.float32
BF16 = jnp.bfloat16

D_MODEL = 2048
GDN_DK = 128
GDN_DV = 128
GDN_W = 768
GDN_HEADS = 6
ATT_HD = 64
ATT_W = 768
ATT_HEADS = 12
CONV_CH = 512
CONV_WIDTH = 31
SHORT_CONV = 4
GDN_CHUNK = 64
ROPE_THETA = 500000.0
ROPE_DIM = 16
DIL_PATTERNS = ((128, 1), (512, 4), (2048, 16))
ATT_BLOCK = 128
NEG_INF = -1e30
LOG2E = 1.4426950408889634
LN2 = 0.6931471805599453
MAIN_W = 2 * CONV_CH + CONV_CH + 4 * GDN_W + 4 * ATT_W
BA_W = 256

COL_GQ, COL_GK, COL_GV, COL_GZ, COL_AQ, COL_AK, COL_AV, COL_AG = 2, 3, 4, 5, 6, 7, 8, 9

LANES = 128
SUBLANES = 8
VMEM_LIMIT = 60 * 1024 * 1024
INPROJ_TM, INPROJ_TN = 2048, 768
INPROJ_LO_TILES = (2 * CONV_CH + CONV_CH + 4 * GDN_W) // INPROJ_TN
OUTPROJ_TM = 512
CONV_TS, CONV_RC, CONV_HALO = 256, 32, 32
CONV_TAPS_PAD = 32
GDN_NB, GDN_TS = 4, 128
GDN_TGROUP = 4
ATT_STEP_BLOCKS = 4
PREP_TS = 512


def _cparams(sem):
    return pltpu.CompilerParams(dimension_semantics=sem, vmem_limit_bytes=VMEM_LIMIT)


def _sigmoid(x):
    return 1.0 / (1.0 + jnp.exp(-x))


def _silu(x):
    return x * _sigmoid(x)


def _dot(a, b):
    return jnp.dot(a, b, preferred_element_type=F32)


def _dot_nt(a, b):
    return lax.dot_general(a, b, (((1,), (1,)), ((), ())), preferred_element_type=F32)


def _rms_norm(x, w):
    ms = jnp.mean(x * x, axis=-1, keepdims=True)
    return (x * lax.rsqrt(ms + 1e-6)) * w


def _prenorm_kernel(x_ref, nw_ref, xn_ref):
    xn_ref[...] = _rms_norm(x_ref[...], nw_ref[...]).astype(BF16)


def _pre_norm(x2, norm_w):
    m = x2.shape[0]
    tm = OUTPROJ_TM
    return pl.pallas_call(
        _prenorm_kernel,
        grid=(m // tm,),
        in_specs=[pl.BlockSpec((tm, D_MODEL), lambda i: (i, 0)), pl.BlockSpec((1, D_MODEL), lambda i: (0, 0))],
        out_specs=pl.BlockSpec((tm, D_MODEL), lambda i: (i, 0)),
        out_shape=jax.ShapeDtypeStruct((m, D_MODEL), BF16),
        compiler_params=_cparams(("arbitrary",)),
        name="pre_norm",
    )(x2, norm_w)


def _inproj_kernel(xn_ref, wlo_ref, whi_ref, wba_ref, u_ref, ba_ref):
    j = pl.program_id(1)

    @pl.when(j == 0)
    def _():
        ba_ref[...] = _dot_nt(xn_ref[...], wba_ref[...])

    @pl.when(j < INPROJ_LO_TILES)
    def _():
        u_ref[...] = _dot_nt(xn_ref[...], wlo_ref[...])

    @pl.when(j >= INPROJ_LO_TILES)
    def _():
        u_ref[...] = _dot_nt(xn_ref[...], whi_ref[...])


def _in_proj(xn, w_lo, w_hi, w_ba, layer):
    m = xn.shape[0]
    tm, tn = INPROJ_TM, INPROJ_TN
    n_lo = INPROJ_LO_TILES
    return pl.pallas_call(
        _inproj_kernel,
        grid=(m // tm, MAIN_W // tn),
        in_specs=[
            pl.BlockSpec((tm, D_MODEL), lambda i, j: (i, 0)),
            pl.BlockSpec((None, tn, D_MODEL), lambda i, j: (layer, jnp.minimum(j, n_lo - 1), 0)),
            pl.BlockSpec((None, tn, D_MODEL), lambda i, j: (layer, jnp.maximum(j - n_lo, 0), 0)),
            pl.BlockSpec((None, BA_W, D_MODEL), lambda i, j: (layer, 0, 0)),
        ],
        out_specs=[
            pl.BlockSpec((tm, tn), lambda i, j: (i, j)),
            pl.BlockSpec((tm, BA_W), lambda i, j: (i, 0)),
        ],
        out_shape=[jax.ShapeDtypeStruct((m, MAIN_W), F32), jax.ShapeDtypeStruct((m, BA_W), F32)],
        compiler_params=_cparams(("arbitrary", "arbitrary")),
        name="in_proj",
    )(xn, w_lo, w_hi, w_ba)


def _conv_kernel(cin_ref, cgate_ref, dww_ref, dwb_ref, lnw_ref, lnb_ref, pw_ref, y_ref, hbuf, hsh, act):
    ts, rc, halo = CONV_TS, CONV_RC, CONV_HALO
    s = pl.program_id(1)

    @pl.when(s == 0)
    def _():
        hbuf[0:halo, :] = jnp.zeros((halo, CONV_CH), F32)

    @pl.when(s > 0)
    def _():
        hbuf[0:halo, :] = hbuf[ts:ts + halo, :]

    u = cin_ref[0]
    hbuf[halo:halo + ts, :] = u[:, :CONV_CH] * _sigmoid(u[:, CONV_CH:])

    for r in range(SUBLANES):
        span = ts + halo - (SUBLANES if r else 0)
        hsh[r, 0:span // SUBLANES] = hbuf[r:r + span, :].reshape(span // SUBLANES, SUBLANES, CONV_CH)

    base = halo - (CONV_WIDTH - 1)
    groups = rc // SUBLANES

    for c in range(ts // rc):
        r0 = c * rc
        acc = jnp.zeros((groups, SUBLANES, CONV_CH), F32) + dwb_ref[...]
        for k in range(CONV_WIDTH):
            shift = (base + k) % SUBLANES
            g0 = (r0 + base + k - shift) // SUBLANES
            acc = acc + dww_ref[k] * hsh[shift, g0:g0 + groups]
        mu = jnp.mean(acc, axis=-1, keepdims=True)
        d = acc - mu
        var = jnp.mean(d * d, axis=-1, keepdims=True)
        hn = d * lax.rsqrt(var + 1e-5) * lnw_ref[...] + lnb_ref[...]
        act[r0:r0 + rc, :] = _silu(hn).reshape(rc, CONV_CH).astype(BF16)

    y = _dot(act[...], pw_ref[...]) * _silu(cgate_ref[0])
    y_ref[0] = y.astype(BF16)


def _conv_branch(u3, dw_w, dw_b, ln_w, ln_b, pw_w):
    b, s, _ = u3.shape
    ts = CONV_TS
    fixed = lambda bi, si: (0, 0)
    return pl.pallas_call(
        _conv_kernel,
        grid=(b, s // ts),
        in_specs=[
            pl.BlockSpec((1, ts, 2 * CONV_CH), lambda bi, si: (bi, si, 0)),
            pl.BlockSpec((1, ts, CONV_CH), lambda bi, si: (bi, si, 2)),
            pl.BlockSpec((CONV_TAPS_PAD, SUBLANES, CONV_CH), lambda bi, si: (0, 0, 0)),
            pl.BlockSpec((1, CONV_CH), fixed),
            pl.BlockSpec((1, CONV_CH), fixed),
            pl.BlockSpec((1, CONV_CH), fixed),
            pl.BlockSpec((CONV_CH, CONV_CH), fixed),
        ],
        out_specs=pl.BlockSpec((1, ts, CONV_CH), lambda bi, si: (bi, si, 0)),
        out_shape=jax.ShapeDtypeStruct((b, s, CONV_CH), BF16),
        scratch_shapes=[pltpu.VMEM((ts + CONV_HALO, CONV_CH), F32),
                        pltpu.VMEM((SUBLANES, (ts + CONV_HALO) // SUBLANES, SUBLANES, CONV_CH), F32),
                        pltpu.VMEM((ts, CONV_CH), BF16)],
        compiler_params=_cparams(("arbitrary", "arbitrary")),
        name="conv_branch",
    )(u3, u3, dw_w, dw_b, ln_w, ln_b, pw_w)


def _split3(x):
    hi = x.astype(BF16)
    r1 = x - hi.astype(F32)
    mid = r1.astype(BF16)
    lo = (r1 - mid.astype(F32)).astype(BF16)
    return hi, mid, lo


def _blockdiag(p, low):
    z = jnp.zeros_like(p)
    return jnp.concatenate([jnp.where(low, p, z), jnp.where(low, z, p)], axis=0)


def _unit_lower_inverse_pairs(a_list, ii, jj, low):
    c = a_list[0].shape[0]
    same = lambda log2: jnp.right_shift(ii, log2) == jnp.right_shift(jj, log2)
    in_blk = same(1)
    rs = [-jnp.where(in_blk, a, 0.0) for a in a_list]
    lg = 1
    while (1 << lg) < c:
        in_pair = same(lg + 1)
        es = [jnp.where(in_pair, jnp.where(in_blk, 0.0, a), 0.0) for a in a_list]
        rbs = [r.astype(BF16) for r in rs]
        xs = [_dot(rb, _blockdiag(e.astype(BF16), low)) for rb, e in zip(rbs, es)]
        ys = [_dot((e + x).astype(BF16), _blockdiag(rb, low)) for e, x, rb in zip(es, xs, rbs)]
        rs = [r - e - x - y for r, e, x, y in zip(rs, es, xs, ys)]
        in_blk = in_pair
        lg += 1
    return rs


def _gdn_kernel(q_ref, k_ref, v_ref, z_ref, ba_ref, cw_ref, gp_ref, nw_ref, tril_ref, y_ref,
                carry, xbuf, qn, kn, vn, gcum, beta, tbuf, qkbuf, kdbuf, obuf, state):
    nb, ts, c = GDN_NB, GDN_TS, GDN_CHUNK
    n_chunks = ts // c
    w3 = 3 * GDN_W
    heads = range(GDN_HEADS)
    hsl = lambda h: slice(h * GDN_DK, (h + 1) * GDN_DK)
    s = pl.program_id(1)

    @pl.when(s == 0)
    def _():
        carry[...] = jnp.zeros(carry.shape, F32)
        state[...] = jnp.zeros(state.shape, F32)

    base = SUBLANES - (SHORT_CONV - 1)
    gsteps = []
    for bb in range(nb):
        rows_b = slice(bb * ts, (bb + 1) * ts)
        xbuf[bb, 0:SUBLANES, :] = carry[bb]
        xbuf[bb, SUBLANES:SUBLANES + ts, 0:GDN_W] = q_ref[bb]
        xbuf[bb, SUBLANES:SUBLANES + ts, GDN_W:2 * GDN_W] = k_ref[bb]
        xbuf[bb, SUBLANES:SUBLANES + ts, 2 * GDN_W:w3] = v_ref[bb]
        carry[bb] = xbuf[bb, ts:ts + SUBLANES, :]
        for part, dst in ((0, qn), (1, kn), (2, vn)):
            for h in heads:
                lo = part * GDN_W + h * GDN_DK
                acc = jnp.zeros((ts, GDN_DK), F32)
                for t in range(SHORT_CONV):
                    acc = acc + cw_ref[t:t + 1, lo:lo + GDN_DK] * xbuf[bb, base + t:base + t + ts, lo:lo + GDN_DK]
                acc = _silu(acc)
                if part == 0:
                    acc = (acc * lax.rsqrt(jnp.sum(acc * acc, axis=-1, keepdims=True) + 1e-6)) * (GDN_DK ** -0.5)
                elif part == 1:
                    acc = acc * lax.rsqrt(jnp.sum(acc * acc, axis=-1, keepdims=True) + 1e-6)
                dst[rows_b, hsl(h)] = acc

        ba = ba_ref[bb]
        beta[rows_b, :] = _sigmoid(ba[:, 0:LANES])
        xa = ba[:, LANES:2 * LANES] + gp_ref[1:2, :]
        softplus = jnp.maximum(xa, 0.0) + jnp.log(1.0 + jnp.exp(-jnp.abs(xa)))
        gsteps.append(-jnp.exp(gp_ref[0:1, :]) * softplus)

    tril = tril_ref[...]
    g_hi, g_mid, g_lo = _split3(jnp.concatenate(gsteps, axis=0))
    gcum[...] = _dot(tril, g_hi) + _dot(tril, g_mid) + _dot(tril, g_lo)

    ii = lax.broadcasted_iota(jnp.int32, (c, 2 * c), 0)
    lane = lax.broadcasted_iota(jnp.int32, (c, 2 * c), 1)
    jj = jnp.bitwise_and(lane, c - 1)
    low = lane < c
    low_row = low[0:1, :]
    causal = ii >= jj
    strict = ii > jj
    eye = jnp.where(ii == jj, 1.0, 0.0)
    pairs = range(GDN_HEADS // 2)

    def t_body(gi, _):
        items = []
        for cc in range(GDN_TGROUP):
            unit = gi * GDN_TGROUP + cc
            rows_u = pl.ds(pl.multiple_of(unit * c, c), c)
            g_blk = gcum[rows_u, :]
            g_t = g_blk.T
            g_t2 = jnp.concatenate([g_t, g_t], axis=1)
            b_blk = beta[rows_u, :]
            for h in heads:
                gr = g_t[h:h + 1, :]
                kdbuf[unit * GDN_HEADS + h] = (kn[rows_u, hsl(h)].T * jnp.exp(gr[:, c - 1:c] - gr)).astype(BF16)
            for p in pairs:
                ha, hb = 2 * p, 2 * p + 1
                gc = jnp.where(low, g_blk[:, ha:ha + 1], g_blk[:, hb:hb + 1])
                gr = jnp.where(low_row, g_t2[ha:ha + 1, :], g_t2[hb:hb + 1, :])
                bc = jnp.where(low, b_blk[:, ha:ha + 1], b_blk[:, hb:hb + 1])
                items.append((unit * (GDN_HEADS // 2) + p, rows_u, ha, hb, gc - gr, bc))
        zero = jnp.zeros((c, GDN_DK), BF16)
        decays, kks, qks = [], [], []
        for _, rows_u, ha, hb, gdiff, _ in items:
            ka, kb = kn[rows_u, hsl(ha)].astype(BF16), kn[rows_u, hsl(hb)].astype(BF16)
            qa, qb = qn[rows_u, hsl(ha)].astype(BF16), qn[rows_u, hsl(hb)].astype(BF16)
            k_diag = jnp.concatenate([jnp.concatenate([ka, zero], axis=1), jnp.concatenate([zero, kb], axis=1)], axis=0)
            decays.append(jnp.where(causal, jnp.exp(jnp.where(causal, gdiff, 0.0)), 0.0))
            kks.append(_dot_nt(jnp.concatenate([ka, kb], axis=1), k_diag))
            qks.append(_dot_nt(jnp.concatenate([qa, qb], axis=1), k_diag))
        a_s = [jnp.where(strict, it[5] * kk * dc, 0.0) for it, kk, dc in zip(items, kks, decays)]
        for it, qk, dc in zip(items, qks, decays):
            qkbuf[it[0]] = _blockdiag(jnp.where(causal, qk * dc, 0.0).astype(BF16), low)
        rs = _unit_lower_inverse_pairs(a_s, ii, jj, low)
        for it, r in zip(items, rs):
            tbuf[it[0]] = _blockdiag((r + eye).astype(BF16), low)
        return 0

    lax.fori_loop(0, nb * n_chunks // GDN_TGROUP, t_body, 0)

    chains = [(bb, h) for bb in range(nb) for h in heads]

    def chunk_body(ci_, _):
        rows, mats, pair_mats, gcs, bcs = [], [], [], [], []
        for bb in range(nb):
            unit = bb * n_chunks + ci_
            rows_u = pl.ds(pl.multiple_of(unit * c, c), c)
            g_blk = gcum[rows_u, :]
            b_blk = beta[rows_u, :]
            pair_mats += [unit * (GDN_HEADS // 2) + p for p in pairs]
            for h in heads:
                rows.append(rows_u)
                mats.append(unit * GDN_HEADS + h)
                gcs.append(g_blk[:, h:h + 1])
                bcs.append(b_blk[:, h:h + 1])
        n = range(len(chains))
        egs = [jnp.exp(gc) for gc in gcs]
        glasts = [gc[c - 1:c, :] for gc in gcs]
        s_hs = [state[bb * GDN_HEADS + h] for bb, h in chains]
        s_bs = [sh.astype(BF16) for sh in s_hs]
        kq_s = [_dot(jnp.concatenate([kn[rows[i], hsl(chains[i][1])].astype(BF16),
                                      (qn[rows[i], hsl(chains[i][1])] * egs[i]).astype(BF16)], axis=0), s_bs[i]) for i in n]
        o_state = [r[c:2 * c, :] for r in kq_s]
        us = [bcs[i] * (vn[rows[i], hsl(chains[i][1])] - egs[i] * kq_s[i][0:c, :]) for i in n]
        half = range(len(chains) // 2)
        v_pairs = [_dot(tbuf[pair_mats[j]],
                        jnp.concatenate([us[2 * j], us[2 * j + 1]], axis=0).astype(BF16)).astype(BF16) for j in half]
        vbs = [v_pairs[i // 2][(i % 2) * c:(i % 2 + 1) * c, :] for i in n]
        for i, (bb, h) in enumerate(chains):
            state[bb * GDN_HEADS + h] = s_hs[i] * jnp.exp(glasts[i]) + _dot(kdbuf[mats[i]], vbs[i])
        o_pairs = [_dot(qkbuf[pair_mats[j]], v_pairs[j]) for j in half]
        for i, (bb, h) in enumerate(chains):
            obuf[rows[i], hsl(h)] = o_state[i] + o_pairs[i // 2][(i % 2) * c:(i % 2 + 1) * c, :]
        return 0

    lax.fori_loop(0, n_chunks, chunk_body, 0)

    for bb in range(nb):
        for h in heads:
            o = obuf[bb * ts:(bb + 1) * ts, hsl(h)]
            on = (o * lax.rsqrt(jnp.mean(o * o, axis=-1, keepdims=True) + 1e-6)) * nw_ref[...]
            y_ref[bb, :, hsl(h)] = (on * _silu(z_ref[bb, :, hsl(h)])).astype(BF16)


def _gdn_branch(u3, ba3, conv_w, gparams, norm_w):
    b, s, _ = u3.shape
    nb, ts, c = GDN_NB, GDN_TS, GDN_CHUNK
    assert b % nb == 0 and (nb * ts // c) % GDN_TGROUP == 0
    fixed = lambda bi, si: (0, 0)
    col = lambda j: (lambda bi, si: (bi, si, j))
    rows = nb * ts
    ri = lax.broadcasted_iota(jnp.int32, (rows, rows), 0)
    ci = lax.broadcasted_iota(jnp.int32, (rows, rows), 1)
    tril = ((ri // c == ci // c) & (ri >= ci)).astype(BF16)
    n_mats = (rows // c) * GDN_HEADS
    return pl.pallas_call(
        _gdn_kernel,
        grid=(b // nb, s // ts),
        in_specs=[
            pl.BlockSpec((nb, ts, GDN_W), col(COL_GQ)),
            pl.BlockSpec((nb, ts, GDN_W), col(COL_GK)),
            pl.BlockSpec((nb, ts, GDN_W), col(COL_GV)),
            pl.BlockSpec((nb, ts, GDN_W), col(COL_GZ)),
            pl.BlockSpec((nb, ts, BA_W), col(0)),
            pl.BlockSpec((SUBLANES, 3 * GDN_W), fixed),
            pl.BlockSpec((SUBLANES, LANES), fixed),
            pl.BlockSpec((1, GDN_DV), fixed),
            pl.BlockSpec((rows, rows), fixed),
        ],
        out_specs=pl.BlockSpec((nb, ts, GDN_W), col(0)),
        out_shape=jax.ShapeDtypeStruct((b, s, GDN_W), BF16),
        scratch_shapes=[
            pltpu.VMEM((nb, SUBLANES, 3 * GDN_W), F32),
            pltpu.VMEM((nb, ts + SUBLANES, 3 * GDN_W), F32),
            pltpu.VMEM((rows, GDN_W), F32),
            pltpu.VMEM((rows, GDN_W), F32),
            pltpu.VMEM((rows, GDN_W), F32),
            pltpu.VMEM((rows, LANES), F32),
            pltpu.VMEM((rows, LANES), F32),
            pltpu.VMEM((n_mats // 2, 2 * c, 2 * c), BF16),
            pltpu.VMEM((n_mats // 2, 2 * c, 2 * c), BF16),
            pltpu.VMEM((n_mats, GDN_DK, c), BF16),
            pltpu.VMEM((rows, GDN_W), F32),
            pltpu.VMEM((nb * GDN_HEADS, GDN_DK, GDN_DV), F32),
        ],
        compiler_params=_cparams(("arbitrary", "arbitrary")),
        name="gdn_branch",
    )(u3, u3, u3, u3, ba3, conv_w, gparams, norm_w, tril)


def _strided_rows(r, n, dil):
    return slice(None) if dil == 1 else pl.ds(r, n, stride=dil)


def _attprep_kernel(q_ref, k_ref, v_ref, c_ref, s1_ref, s2_ref, *rest):
    outs, (qs, ks, vs) = rest[:-3], rest[-3:]
    ts = PREP_TS
    cs, s1, s2 = c_ref[...], s1_ref[...], s2_ref[...]
    for p in range(ATT_W // LANES):
        sl = slice(p * LANES, (p + 1) * LANES)
        for src, dst, scale in ((q_ref, qs, LOG2E * ATT_HD ** -0.5), (k_ref, ks, None)):
            x = src[0, :, sl]
            r = x * cs + pltpu.roll(x, LANES - ROPE_DIM // 2, axis=1) * s1 + pltpu.roll(x, ROPE_DIM // 2, axis=1) * s2
            dst[p] = r if scale is None else r * scale
        vs[p] = v_ref[0, :, sl]
    for di, (_, dil) in enumerate(DIL_PATTERNS):
        n = ts // dil
        for r in range(dil):
            rows = _strided_rows(r, n, dil)
            for p in range(ATT_W // LANES):
                sl = slice(p * LANES, (p + 1) * LANES)
                for dst, src in zip(outs[3 * di:3 * di + 3], (qs, ks, vs)):
                    dst[0, r, :, sl] = src[p, rows, :].astype(BF16)


def _att_prep(u3, cos_t, sin_a, sin_b):
    b, s, _ = u3.shape
    ts = PREP_TS
    col = lambda j: (lambda bi, si: (bi, si, j))
    tab = pl.BlockSpec((ts, LANES), lambda bi, si: (si, 0))
    out_specs, out_shape = [], []
    for _, dil in DIL_PATTERNS:
        out_specs += [pl.BlockSpec((1, dil, ts // dil, ATT_W), lambda bi, si: (bi, 0, si, 0))] * 3
        out_shape += [jax.ShapeDtypeStruct((b, dil, s // dil, ATT_W), BF16)] * 3
    return pl.pallas_call(
        _attprep_kernel,
        grid=(b, s // ts),
        in_specs=[pl.BlockSpec((1, ts, ATT_W), col(COL_AQ)), pl.BlockSpec((1, ts, ATT_W), col(COL_AK)),
                  pl.BlockSpec((1, ts, ATT_W), col(COL_AV)), tab, tab, tab],
        out_specs=out_specs,
        out_shape=out_shape,
        scratch_shapes=[pltpu.VMEM((ATT_W // LANES, ts, LANES), F32)] * 3,
        compiler_params=_cparams(("arbitrary", "arbitrary")),
        name="att_prep",
    )(u3, u3, u3, cos_t, sin_a, sin_b)


def _att_kernel(q_ref, k_ref, v_ref, o_ref, lse_ref, kprev, vprev):
    blk = ATT_BLOCK
    n = pl.program_id(2)

    @pl.when(n == 0)
    def _():
        kprev[...] = jnp.zeros(kprev.shape, BF16)
        vprev[...] = jnp.zeros(vprev.shape, BF16)

    row = lax.broadcasted_iota(jnp.int32, (blk, 2 * blk), 0)
    col = lax.broadcasted_iota(jnp.int32, (blk, 2 * blk), 1)
    first = jnp.where(n > 0, 0, 4 * blk)
    ahead = col - row
    lane = lax.broadcasted_iota(jnp.int32, (blk, LANES), 1)
    low = lane < ATT_HD

    for j in range(ATT_STEP_BLOCKS):
        rows = slice(j * blk, (j + 1) * blk)
        if j == 0:
            k_before, v_before = kprev, vprev
            valid = jnp.where(col < blk, ahead - first, blk - ahead) >= 0
        else:
            k_before, v_before = k_ref.at[(j - 1) * blk:j * blk], v_ref.at[(j - 1) * blk:j * blk]
            valid = jnp.where(col < blk, ahead, blk - ahead) >= 0

        scores = []
        for p in range(ATT_W // LANES):
            sl = slice(p * LANES, (p + 1) * LANES)
            kcat = jnp.concatenate([k_before[:, sl], k_ref[rows, sl]], axis=0)
            qp = q_ref[rows, sl]
            for hh in range(2):
                hm = low if hh == 0 else jnp.logical_not(low)
                scores.append(_dot_nt(jnp.where(hm, qp, jnp.zeros_like(qp)), kcat))

        lse_tile = jnp.zeros((blk, LANES), F32)
        for p in range(ATT_W // LANES):
            sl = slice(p * LANES, (p + 1) * LANES)
            vcat = jnp.concatenate([v_before[:, sl], v_ref[rows, sl]], axis=0)
            o_pair = None
            for hh in range(2):
                sc = jnp.where(valid, scores[2 * p + hh], NEG_INF)
                m = jnp.max(sc, axis=-1, keepdims=True)
                pe = jnp.exp2(sc - m)
                l = jnp.sum(pe, axis=-1, keepdims=True)
                o = _dot(pe.astype(BF16), vcat) * (1.0 / l)
                o_pair = o if hh == 0 else jnp.where(low, o_pair, o)
                lse_tile = jnp.where(lane == 2 * p + hh, m * LN2 + jnp.log(l), lse_tile)
            o_ref[rows, sl] = o_pair
        lse_ref[rows, :] = lse_tile

    last = slice((ATT_STEP_BLOCKS - 1) * blk, ATT_STEP_BLOCKS * blk)
    kprev[...] = k_ref[last, :]
    vprev[...] = v_ref[last, :]


def _att_pattern(qd, kd, vd):
    b, dil, lr, _ = qd.shape
    blk = ATT_STEP_BLOCKS * ATT_BLOCK
    idx = lambda bi, r, n: (bi, r, n, 0)
    spec = pl.BlockSpec((None, None, blk, ATT_W), idx)
    return pl.pallas_call(
        _att_kernel,
        grid=(b, dil, lr // blk),
        in_specs=[spec, spec, spec],
        out_specs=[spec, pl.BlockSpec((None, None, blk, LANES), idx)],
        out_shape=[jax.ShapeDtypeStruct((b, dil, lr, ATT_W), F32),
                   jax.ShapeDtypeStruct((b, dil, lr, LANES), F32)],
        scratch_shapes=[pltpu.VMEM((ATT_BLOCK, ATT_W), BF16), pltpu.VMEM((ATT_BLOCK, ATT_W), BF16)],
        compiler_params=_cparams(("arbitrary", "arbitrary", "arbitrary")),
        name=f"att_dil{dil}",
    )(qd, kd, vd)


def _combine_patterns(o_refs, l_refs, g_ref, e_ref, ya, oscr, lscr):
    tm = OUTPROJ_TM
    for g, (o_ref, l_ref) in enumerate(zip(o_refs[1:], l_refs[1:])):
        dil = DIL_PATTERNS[g + 1][1]
        n = tm // dil
        for r in range(dil):
            rows = pl.ds(r, n, stride=dil)
            lscr[g, rows, :] = l_ref[r]
            for p in range(ATT_W // LANES):
                oscr[g, p, rows, :] = o_ref[r, :, p * LANES:(p + 1) * LANES]
    l1, l2, l3 = l_refs[0][0], lscr[0], lscr[1]
    mx = jnp.maximum(jnp.maximum(l1, l2), l3)
    e1, e2, e3 = jnp.exp(l1 - mx), jnp.exp(l2 - mx), jnp.exp(l3 - mx)
    inv = 1.0 / (e1 + e2 + e3)
    spread = e_ref[...]
    ws = []
    for w in (e1 * inv, e2 * inv, e3 * inv):
        hi = w.astype(BF16)
        lo = (w - hi.astype(F32)).astype(BF16)
        ws.append(_dot(jnp.concatenate([hi, lo], axis=1), spread))
    for p in range(ATT_W // LANES):
        sl = slice(p * LANES, (p + 1) * LANES)
        tiles = (o_refs[0][0, :, sl], oscr[0, p], oscr[1, p])
        acc = None
        for w, o in zip(ws, tiles):
            t = w[:, sl] * o
            acc = t if acc is None else acc + t
        ya[:, sl] = (acc * _silu(g_ref[:, sl])).astype(BF16)


def _outproj_kernel(yc_ref, yg_ref, o1_ref, o2_ref, o3_ref, l1_ref, l2_ref, l3_ref, g_ref, e_ref, x_ref,
                    wc_ref, wg_ref, wa_ref, nw_ref, o_ref, *rest, final):
    ya, oscr, lscr = rest[-3:]
    acc = _dot(yc_ref[...], wc_ref[...]) + _dot(yg_ref[...], wg_ref[...])
    _combine_patterns((o1_ref, o2_ref, o3_ref), (l1_ref, l2_ref, l3_ref), g_ref, e_ref, ya, oscr, lscr)
    xo = x_ref[...] + (acc + _dot(ya[...], wa_ref[...]))
    if final:
        o_ref[...] = _rms_norm(xo, nw_ref[...])
    else:
        o_ref[...] = xo
        rest[0][...] = _rms_norm(xo, nw_ref[...]).astype(BF16)


def _out_proj(y_conv, y_gdn, outs, lses, u2, x2, w_c, w_g, w_a, layer, next_norm_w, final):
    m = x2.shape[0]
    tm = OUTPROJ_TM
    tiles_per_seq = outs[0].shape[2] // tm
    row = lambda i: (i, 0)
    fixed = lambda i: (0, 0)
    of_layer = lambda i: (layer, 0, 0)
    of_seq = lambda i: (i // tiles_per_seq, 0, i % tiles_per_seq, 0)
    ospecs = [pl.BlockSpec((None, dil, tm // dil, ATT_W), of_seq) for _, dil in DIL_PATTERNS]
    lspecs = [pl.BlockSpec((None, dil, tm // dil, LANES), of_seq) for _, dil in DIL_PATTERNS]
    head_of_lane = lax.broadcasted_iota(jnp.int32, (2 * LANES, ATT_W), 1) // ATT_HD
    spread = (head_of_lane == lax.broadcasted_iota(jnp.int32, (2 * LANES, ATT_W), 0) % LANES).astype(BF16)
    out_specs = [pl.BlockSpec((tm, D_MODEL), row)]
    out_shape = [jax.ShapeDtypeStruct((m, D_MODEL), F32)]
    if not final:
        out_specs.append(pl.BlockSpec((tm, D_MODEL), row))
        out_shape.append(jax.ShapeDtypeStruct((m, D_MODEL), BF16))
    return pl.pallas_call(
        functools.partial(_outproj_kernel, final=final),
        grid=(m // tm,),
        in_specs=[
            pl.BlockSpec((tm, CONV_CH), row),
            pl.BlockSpec((tm, GDN_W), row),
            *ospecs, *lspecs,
            pl.BlockSpec((tm, ATT_W), lambda i: (i, COL_AG)),
            pl.BlockSpec((2 * LANES, ATT_W), fixed),
            pl.BlockSpec((tm, D_MODEL), row),
            pl.BlockSpec((None, CONV_CH, D_MODEL), of_layer),
            pl.BlockSpec((None, GDN_W, D_MODEL), of_layer),
            pl.BlockSpec((None, ATT_W, D_MODEL), of_layer),
            pl.BlockSpec((1, D_MODEL), fixed),
        ],
        out_specs=out_specs,
        out_shape=out_shape,
        scratch_shapes=[pltpu.VMEM((tm, ATT_W), BF16),
                        pltpu.VMEM((2, ATT_W // LANES, tm, LANES), F32),
                        pltpu.VMEM((2, tm, LANES), F32)],
        compiler_params=_cparams(("arbitrary",)),
        name="out_proj_final" if final else "out_proj",
    )(y_conv, y_gdn, *outs, *lses, u2, spread, x2, w_c, w_g, w_a, next_norm_w)


def _rope_lane_tables(s):
    half = ROPE_DIM // 2
    inv = ROPE_THETA ** (-jnp.arange(half, dtype=F32) / half)
    ang = jnp.arange(s, dtype=F32)[:, None] * inv[None, :]
    cos, sin = jnp.cos(ang), jnp.sin(ang)
    ones = jnp.ones((s, ATT_HD - ROPE_DIM), F32)
    zeros = jnp.zeros((s, ATT_HD - half), F32)
    cos_h = jnp.concatenate([cos, cos, ones], axis=-1)
    sin_a = jnp.concatenate([-sin, zeros], axis=-1)
    sin_b = jnp.concatenate([jnp.zeros((s, half), F32), sin, zeros[:, half:]], axis=-1)
    rep = lambda t: jnp.concatenate([t, t], axis=-1)
    return rep(cos_h), rep(sin_a), rep(sin_b)


def kernel(x, norm_w, w_in, conv_qkv_w, a_log, dt_bias, gdn_norm_w, conf_dw_w, conf_dw_b, conf_ln_w,
           conf_ln_b, conf_pw_w, w_out, final_norm_w):
    b, s, d = x.shape
    depth = w_in.shape[0]
    m = b * s
    assert d == D_MODEL and m % INPROJ_TM == 0 and s % (DIL_PATTERNS[-1][1] * ATT_BLOCK * ATT_STEP_BLOCKS) == 0

    ba0 = 2 * CONV_CH + CONV_CH + 4 * GDN_W
    w_t = jnp.transpose(w_in, (0, 2, 1))
    w_lo = w_t[:, :ba0].astype(BF16)
    w_hi = w_t[:, ba0 + 2 * GDN_HEADS:].astype(BF16)
    w_ba = jnp.zeros((depth, BA_W, d), F32)
    w_ba = w_ba.at[:, 0:GDN_HEADS].set(w_t[:, ba0:ba0 + GDN_HEADS])
    w_ba = w_ba.at[:, LANES:LANES + GDN_HEADS].set(w_t[:, ba0 + GDN_HEADS:ba0 + 2 * GDN_HEADS])
    w_ba = w_ba.astype(BF16)
    w_out_b = w_out.astype(BF16)
    w_out_c, w_out_g, w_out_a = (w_out_b[:, :CONV_CH], w_out_b[:, CONV_CH:CONV_CH + GDN_W],
                                 w_out_b[:, CONV_CH + GDN_W:])
    pw_b = conf_pw_w.astype(BF16)
    dw_w = jnp.pad(conf_dw_w, ((0, 0), (0, CONV_TAPS_PAD - CONV_WIDTH), (0, 0)))
    dw_w = jnp.broadcast_to(dw_w[:, :, None, :], (depth, CONV_TAPS_PAD, SUBLANES, CONV_CH))
    cw = jnp.pad(conv_qkv_w, ((0, 0), (0, SUBLANES - SHORT_CONV), (0, 0)))
    gparams = jnp.zeros((depth, SUBLANES, LANES), F32)
    gparams = gparams.at[:, 0, 0:GDN_HEADS].set(a_log).at[:, 1, 0:GDN_HEADS].set(dt_bias)
    cos_t, sin_a, sin_b = _rope_lane_tables(s)

    x2 = x.reshape(m, d)
    xn = _pre_norm(x2, norm_w[0][None, :])
    for l in range(depth):
        last = l == depth - 1
        u, ba = _in_proj(xn, w_lo, w_hi, w_ba, l)
        u3 = u.reshape(b, s, MAIN_W)
        y_conv = _conv_branch(u3, dw_w[l], conf_dw_b[l][None, :], conf_ln_w[l][None, :], conf_ln_b[l][None, :],
                              pw_b[l])
        y_gdn = _gdn_branch(u3, ba.reshape(b, s, BA_W), cw[l], gparams[l], gdn_norm_w[l][None, :])
        qkv = _att_prep(u3, cos_t, sin_a, sin_b)
        outs, lses = [], []
        for g in range(len(DIL_PATTERNS)):
            o_g, lse_g = _att_pattern(*qkv[3 * g:3 * g + 3])
            outs.append(o_g)
            lses.append(lse_g)
        next_w = final_norm_w if last else norm_w[l + 1]
        res = _out_proj(y_conv.reshape(m, CONV_CH), y_gdn.reshape(m, GDN_W), outs, lses, u, x2,
                        w_out_c, w_out_g, w_out_a, l, next_w[None, :], final=last)
        x2 = res[0]
        xn = None if last else res[1]
    return x2.reshape(b, s, d)
```

```python
import functools

import jax
import jax.numpy as jnp
from jax import lax
from jax.experimental import pallas as pl
from jax.experimental.pallas import tpu as pltpu

F32 = jnp.float32
BF16 = jnp.bfloat16

D_MODEL = 2048
GDN_DK = 128
GDN_DV = 128
GDN_W = 768
GDN_HEADS = 6
ATT_HD = 64
ATT_W = 768
ATT_HEADS = 12
CONV_CH = 512
CONV_WIDTH = 31
SHORT_CONV = 4
GDN_CHUNK = 64
ROPE_THETA = 500000.0
ROPE_DIM = 16
DIL_PATTERNS = ((128, 1), (512, 4), (2048, 16))
ATT_BLOCK = 128
NEG_INF = -1e30
LOG2E = 1.4426950408889634
LN2 = 0.6931471805599453
MAIN_W = 2 * CONV_CH + CONV_CH + 4 * GDN_W + 4 * ATT_W
BA_W = 256

COL_GQ, COL_GK, COL_GV, COL_GZ, COL_AQ, COL_AK, COL_AV, COL_AG = 2, 3, 4, 5, 6, 7, 8, 9

LANES = 128
SUBLANES = 8
VMEM_LIMIT = 60 * 1024 * 1024
INPROJ_TM, INPROJ_TN = 2048, 768
OUTPROJ_TM = 512
CONV_TS, CONV_RC, CONV_HALO = 256, 32, 32
CONV_TAPS_PAD = 32
GDN_NB, GDN_TS = 4, 128
GDN_HALO = 16
GDN_TGROUP = 4
ATT_STEP_BLOCKS = 4
PREP_TS = 512


def _cparams(sem):
    return pltpu.CompilerParams(dimension_semantics=sem, vmem_limit_bytes=VMEM_LIMIT)


def _sigmoid(x):
    return 1.0 / (1.0 + jnp.exp(-x))


def _silu(x):
    return x * _sigmoid(x)


def _dot(a, b):
    return jnp.dot(a, b, preferred_element_type=F32)


def _dot_nt(a, b):
    return lax.dot_general(a, b, (((1,), (1,)), ((), ())), preferred_element_type=F32)


def _rms_norm(x, w):
    ms = jnp.mean(x * x, axis=-1, keepdims=True)
    return (x * lax.rsqrt(ms + 1e-6)) * w


def _prenorm_kernel(x_ref, nw_ref, xn_ref):
    xn_ref[...] = _rms_norm(x_ref[...], nw_ref[...]).astype(BF16)


def _pre_norm(x2, norm_w):
    m = x2.shape[0]
    tm = OUTPROJ_TM
    return pl.pallas_call(
        _prenorm_kernel,
        grid=(m // tm,),
        in_specs=[pl.BlockSpec((tm, D_MODEL), lambda i: (i, 0)), pl.BlockSpec((1, D_MODEL), lambda i: (0, 0))],
        out_specs=pl.BlockSpec((tm, D_MODEL), lambda i: (i, 0)),
        out_shape=jax.ShapeDtypeStruct((m, D_MODEL), BF16),
        compiler_params=_cparams(("arbitrary",)),
        name="pre_norm",
    )(x2, norm_w)


def _inproj_kernel(xn_ref, w_ref, wba_ref, u_ref, ba_ref):
    @pl.when(pl.program_id(1) == 0)
    def _():
        ba_ref[...] = _dot_nt(xn_ref[...], wba_ref[...])

    u_ref[...] = _dot_nt(xn_ref[...], w_ref[...])


def _in_proj(xn, w_main, w_ba, layer):
    m = xn.shape[0]
    tm, tn = INPROJ_TM, INPROJ_TN
    return pl.pallas_call(
        _inproj_kernel,
        grid=(m // tm, MAIN_W // tn),
        in_specs=[
            pl.BlockSpec((tm, D_MODEL), lambda i, j: (i, 0)),
            pl.BlockSpec((None, tn, D_MODEL), lambda i, j: (layer, j, 0)),
            pl.BlockSpec((None, BA_W, D_MODEL), lambda i, j: (layer, 0, 0)),
        ],
        out_specs=[
            pl.BlockSpec((tm, tn), lambda i, j: (i, j)),
            pl.BlockSpec((tm, BA_W), lambda i, j: (i, 0)),
        ],
        out_shape=[jax.ShapeDtypeStruct((m, MAIN_W), F32), jax.ShapeDtypeStruct((m, BA_W), F32)],
        compiler_params=_cparams(("arbitrary", "arbitrary")),
        name="in_proj",
    )(xn, w_main, w_ba)


def _conv_kernel(cin_ref, cgate_ref, dww_ref, dwb_ref, lnw_ref, lnb_ref, pw_ref, y_ref, hbuf, hsh, act):
    ts, rc, halo = CONV_TS, CONV_RC, CONV_HALO
    s = pl.program_id(1)

    @pl.when(s == 0)
    def _():
        hbuf[0:halo, :] = jnp.zeros((halo, CONV_CH), F32)

    @pl.when(s > 0)
    def _():
        hbuf[0:halo, :] = hbuf[ts:ts + halo, :]

    u = cin_ref[0]
    hbuf[halo:halo + ts, :] = u[:, :CONV_CH] * _sigmoid(u[:, CONV_CH:])

    for r in range(SUBLANES):
        span = ts + halo - (SUBLANES if r else 0)
        hsh[r, 0:span // SUBLANES] = hbuf[r:r + span, :].reshape(span // SUBLANES, SUBLANES, CONV_CH)

    base = halo - (CONV_WIDTH - 1)
    groups = rc // SUBLANES

    for c in range(ts // rc):
        r0 = c * rc
        acc = jnp.zeros((groups, SUBLANES, CONV_CH), F32) + dwb_ref[...]
        for k in range(CONV_WIDTH):
            shift = (base + k) % SUBLANES
            g0 = (r0 + base + k - shift) // SUBLANES
            acc = acc + dww_ref[k] * hsh[shift, g0:g0 + groups]
        mu = jnp.mean(acc, axis=-1, keepdims=True)
        d = acc - mu
        var = jnp.mean(d * d, axis=-1, keepdims=True)
        hn = d * lax.rsqrt(var + 1e-5) * lnw_ref[...] + lnb_ref[...]
        act[r0:r0 + rc, :] = _silu(hn).reshape(rc, CONV_CH).astype(BF16)

    y = _dot(act[...], pw_ref[...]) * _silu(cgate_ref[0])
    y_ref[0] = y.astype(BF16)


def _conv_branch(u3, dw_w, dw_b, ln_w, ln_b, pw_w):
    b, s, _ = u3.shape
    ts = CONV_TS
    fixed = lambda bi, si: (0, 0)
    return pl.pallas_call(
        _conv_kernel,
        grid=(b, s // ts),
        in_specs=[
            pl.BlockSpec((1, ts, 2 * CONV_CH), lambda bi, si: (bi, si, 0)),
            pl.BlockSpec((1, ts, CONV_CH), lambda bi, si: (bi, si, 2)),
            pl.BlockSpec((CONV_TAPS_PAD, SUBLANES, CONV_CH), lambda bi, si: (0, 0, 0)),
            pl.BlockSpec((1, CONV_CH), fixed),
            pl.BlockSpec((1, CONV_CH), fixed),
            pl.BlockSpec((1, CONV_CH), fixed),
            pl.BlockSpec((CONV_CH, CONV_CH), fixed),
        ],
        out_specs=pl.BlockSpec((1, ts, CONV_CH), lambda bi, si: (bi, si, 0)),
        out_shape=jax.ShapeDtypeStruct((b, s, CONV_CH), BF16),
        scratch_shapes=[pltpu.VMEM((ts + CONV_HALO, CONV_CH), F32),
                        pltpu.VMEM((SUBLANES, (ts + CONV_HALO) // SUBLANES, SUBLANES, CONV_CH), F32),
                        pltpu.VMEM((ts, CONV_CH), BF16)],
        compiler_params=_cparams(("arbitrary", "arbitrary")),
        name="conv_branch",
    )(u3, u3, dw_w, dw_b, ln_w, ln_b, pw_w)


def _split3(x):
    hi = x.astype(BF16)
    r1 = x - hi.astype(F32)
    mid = r1.astype(BF16)
    lo = (r1 - mid.astype(F32)).astype(BF16)
    return hi, mid, lo


def _blockdiag(p, low):
    z = jnp.zeros_like(p)
    return jnp.concatenate([jnp.where(low, p, z), jnp.where(low, z, p)], axis=0)


def _unit_lower_inverse_pairs(a_list, ii, jj, low):
    c = a_list[0].shape[0]
    same = lambda log2: jnp.right_shift(ii, log2) == jnp.right_shift(jj, log2)
    in_blk = same(1)
    rs = [-jnp.where(in_blk, a, 0.0) for a in a_list]
    lg = 1
    while (1 << lg) < c:
        in_pair = same(lg + 1)
        es = [jnp.where(in_pair, jnp.where(in_blk, 0.0, a), 0.0) for a in a_list]
        rbs = [r.astype(BF16) for r in rs]
        xs = [_dot(rb, _blockdiag(e.astype(BF16), low)) for rb, e in zip(rbs, es)]
        ys = [_dot((e + x).astype(BF16), _blockdiag(rb, low)) for e, x, rb in zip(es, xs, rbs)]
        rs = [r - e - x - y for r, e, x, y in zip(rs, es, xs, ys)]
        in_blk = in_pair
        lg += 1
    return rs


def _gdn_kernel(q_ref, k_ref, v_ref, z_ref, ba_ref, cw_ref, gp_ref, nw_ref, tril_ref, y_ref,
                carry, xbuf, qn, kn, vn, gcum, beta, tbuf, qkbuf, kdbuf, obuf, state):
    nb, ts, c = GDN_NB, GDN_TS, GDN_CHUNK
    n_chunks = ts // c
    w3 = 3 * GDN_W
    heads = range(GDN_HEADS)
    hsl = lambda h: slice(h * GDN_DK, (h + 1) * GDN_DK)
    s = pl.program_id(1)

    @pl.when(s == 0)
    def _():
        carry[...] = jnp.zeros(carry.shape, F32)
        state[...] = jnp.zeros(state.shape, F32)

    assert SHORT_CONV == 4
    hal = GDN_HALO
    gsteps = []
    for bb in range(nb):
        rows_b = slice(bb * ts, (bb + 1) * ts)
        xbuf[bb, 0:hal, :] = carry[bb]
        xbuf[bb, hal:hal + ts, 0:GDN_W] = q_ref[bb]
        xbuf[bb, hal:hal + ts, GDN_W:2 * GDN_W] = k_ref[bb]
        xbuf[bb, hal:hal + ts, 2 * GDN_W:w3] = v_ref[bb]
        carry[bb] = xbuf[bb, ts:ts + hal, :]
        for part, dst in ((0, qn), (1, kn), (2, vn)):
            for h in heads:
                lo = part * GDN_W + h * GDN_DK
                cols = slice(lo, lo + GDN_DK)
                w0, w1, w2, w3_ = (cw_ref[t:t + 1, cols] for t in range(SHORT_CONV))
                x_now = xbuf[bb, hal - SUBLANES:hal + ts, cols]
                x_m2 = xbuf[bb, hal - SUBLANES - 2:hal + ts - 2, cols]
                q_ext = w2 * x_now + w0 * x_m2
                acc = (w3_ * x_now[SUBLANES:] + w1 * x_m2[SUBLANES:]) + q_ext[SUBLANES - 1:SUBLANES - 1 + ts]
                acc = _silu(acc)
                if part == 0:
                    acc = (acc * lax.rsqrt(jnp.sum(acc * acc, axis=-1, keepdims=True) + 1e-6)) * (GDN_DK ** -0.5)
                elif part == 1:
                    acc = acc * lax.rsqrt(jnp.sum(acc * acc, axis=-1, keepdims=True) + 1e-6)
                dst[rows_b, hsl(h)] = acc

        ba = ba_ref[bb]
        beta[rows_b, :] = _sigmoid(ba[:, 0:LANES])
        xa = ba[:, LANES:2 * LANES] + gp_ref[1:2, :]
        softplus = jnp.maximum(xa, 0.0) + jnp.log(1.0 + jnp.exp(-jnp.abs(xa)))
        gsteps.append(-jnp.exp(gp_ref[0:1, :]) * softplus)

    tril = tril_ref[...]
    g_hi, g_mid, g_lo = _split3(jnp.concatenate(gsteps, axis=0))
    gcum[...] = _dot(tril, g_hi) + _dot(tril, g_mid) + _dot(tril, g_lo)

    ii = lax.broadcasted_iota(jnp.int32, (c, 2 * c), 0)
    lane = lax.broadcasted_iota(jnp.int32, (c, 2 * c), 1)
    jj = jnp.bitwise_and(lane, c - 1)
    low = lane < c
    low_row = low[0:1, :]
    causal = ii >= jj
    strict = ii > jj
    eye = jnp.where(ii == jj, 1.0, 0.0)
    pairs = range(GDN_HEADS // 2)

    def t_body(gi, _):
        items = []
        for cc in range(GDN_TGROUP):
            unit = gi * GDN_TGROUP + cc
            rows_u = pl.ds(pl.multiple_of(unit * c, c), c)
            g_blk = gcum[rows_u, :]
            g_t = g_blk.T
            g_t2 = jnp.concatenate([g_t, g_t], axis=1)
            b_blk = beta[rows_u, :]
            for h in heads:
                gr = g_t[h:h + 1, :]
                kdbuf[unit * GDN_HEADS + h] = (kn[rows_u, hsl(h)].T * jnp.exp(gr[:, c - 1:c] - gr)).astype(BF16)
            for p in pairs:
                ha, hb = 2 * p, 2 * p + 1
                gc = jnp.where(low, g_blk[:, ha:ha + 1], g_blk[:, hb:hb + 1])
                gr = jnp.where(low_row, g_t2[ha:ha + 1, :], g_t2[hb:hb + 1, :])
                bc = jnp.where(low, b_blk[:, ha:ha + 1], b_blk[:, hb:hb + 1])
                items.append((unit * (GDN_HEADS // 2) + p, rows_u, ha, hb, gc - gr, bc))
        zero = jnp.zeros((c, GDN_DK), BF16)
        decays, kks, qks = [], [], []
        for _, rows_u, ha, hb, gdiff, _ in items:
            ka, kb = kn[rows_u, hsl(ha)].astype(BF16), kn[rows_u, hsl(hb)].astype(BF16)
            qa, qb = qn[rows_u, hsl(ha)].astype(BF16), qn[rows_u, hsl(hb)].astype(BF16)
            k_diag = jnp.concatenate([jnp.concatenate([ka, zero], axis=1), jnp.concatenate([zero, kb], axis=1)], axis=0)
            decays.append(jnp.where(causal, jnp.exp(jnp.where(causal, gdiff, 0.0)), 0.0))
            kks.append(_dot_nt(jnp.concatenate([ka, kb], axis=1), k_diag))
            qks.append(_dot_nt(jnp.concatenate([qa, qb], axis=1), k_diag))
        a_s = [jnp.where(strict, it[5] * kk * dc, 0.0) for it, kk, dc in zip(items, kks, decays)]
        for it, qk, dc in zip(items, qks, decays):
            qkbuf[it[0]] = _blockdiag(jnp.where(causal, qk * dc, 0.0).astype(BF16), low)
        rs = _unit_lower_inverse_pairs(a_s, ii, jj, low)
        for it, r in zip(items, rs):
            tbuf[it[0]] = _blockdiag((r + eye).astype(BF16), low)
        return 0

    lax.fori_loop(0, nb * n_chunks // GDN_TGROUP, t_body, 0)

    chains = [(bb, h) for bb in range(nb) for h in heads]

    def chunk_body(ci_, _):
        rows, mats, pair_mats, gcs, bcs = [], [], [], [], []
        for bb in range(nb):
            unit = bb * n_chunks + ci_
            rows_u = pl.ds(pl.multiple_of(unit * c, c), c)
            g_blk = gcum[rows_u, :]
            b_blk = beta[rows_u, :]
            pair_mats += [unit * (GDN_HEADS // 2) + p for p in pairs]
            for h in heads:
                rows.append(rows_u)
                mats.append(unit * GDN_HEADS + h)
                gcs.append(g_blk[:, h:h + 1])
                bcs.append(b_blk[:, h:h + 1])
        n = range(len(chains))
        egs = [jnp.exp(gc) for gc in gcs]
        glasts = [gc[c - 1:c, :] for gc in gcs]
        s_hs = [state[bb * GDN_HEADS + h] for bb, h in chains]
        s_bs = [sh.astype(BF16) for sh in s_hs]
        kq_s = [_dot(jnp.concatenate([kn[rows[i], hsl(chains[i][1])].astype(BF16),
                                      (qn[rows[i], hsl(chains[i][1])] * egs[i]).astype(BF16)], axis=0), s_bs[i]) for i in n]
        o_state = [r[c:2 * c, :] for r in kq_s]
        us = [bcs[i] * (vn[rows[i], hsl(chains[i][1])] - egs[i] * kq_s[i][0:c, :]) for i in n]
        half = range(len(chains) // 2)
        v_pairs = [_dot(tbuf[pair_mats[j]],
                        jnp.concatenate([us[2 * j], us[2 * j + 1]], axis=0).astype(BF16)).astype(BF16) for j in half]
        vbs = [v_pairs[i // 2][(i % 2) * c:(i % 2 + 1) * c, :] for i in n]
        for i, (bb, h) in enumerate(chains):
            state[bb * GDN_HEADS + h] = s_hs[i] * jnp.exp(glasts[i]) + _dot(kdbuf[mats[i]], vbs[i])
        o_pairs = [_dot(qkbuf[pair_mats[j]], v_pairs[j]) for j in half]
        for i, (bb, h) in enumerate(chains):
            obuf[rows[i], hsl(h)] = o_state[i] + o_pairs[i // 2][(i % 2) * c:(i % 2 + 1) * c, :]
        return 0

    lax.fori_loop(0, n_chunks, chunk_body, 0)

    for bb in range(nb):
        for h in heads:
            o = obuf[bb * ts:(bb + 1) * ts, hsl(h)]
            on = (o * lax.rsqrt(jnp.mean(o * o, axis=-1, keepdims=True) + 1e-6)) * nw_ref[...]
            y_ref[bb, :, hsl(h)] = (on * _silu(z_ref[bb, :, hsl(h)])).astype(BF16)


def _gdn_branch(u3, ba3, conv_w, gparams, norm_w):
    b, s, _ = u3.shape
    nb, ts, c = GDN_NB, GDN_TS, GDN_CHUNK
    assert b % nb == 0 and (nb * ts // c) % GDN_TGROUP == 0
    fixed = lambda bi, si: (0, 0)
    col = lambda j: (lambda bi, si: (bi, si, j))
    rows = nb * ts
    ri = lax.broadcasted_iota(jnp.int32, (rows, rows), 0)
    ci = lax.broadcasted_iota(jnp.int32, (rows, rows), 1)
    tril = ((ri // c == ci // c) & (ri >= ci)).astype(BF16)
    n_mats = (rows // c) * GDN_HEADS
    return pl.pallas_call(
        _gdn_kernel,
        grid=(b // nb, s // ts),
        in_specs=[
            pl.BlockSpec((nb, ts, GDN_W), col(COL_GQ)),
            pl.BlockSpec((nb, ts, GDN_W), col(COL_GK)),
            pl.BlockSpec((nb, ts, GDN_W), col(COL_GV)),
            pl.BlockSpec((nb, ts, GDN_W), col(COL_GZ)),
            pl.BlockSpec((nb, ts, BA_W), col(0)),
            pl.BlockSpec((SUBLANES, 3 * GDN_W), fixed),
            pl.BlockSpec((SUBLANES, LANES), fixed),
            pl.BlockSpec((1, GDN_DV), fixed),
            pl.BlockSpec((rows, rows), fixed),
        ],
        out_specs=pl.BlockSpec((nb, ts, GDN_W), col(0)),
        out_shape=jax.ShapeDtypeStruct((b, s, GDN_W), BF16),
        scratch_shapes=[
            pltpu.VMEM((nb, GDN_HALO, 3 * GDN_W), F32),
            pltpu.VMEM((nb, ts + GDN_HALO, 3 * GDN_W), F32),
            pltpu.VMEM((rows, GDN_W), F32),
            pltpu.VMEM((rows, GDN_W), F32),
            pltpu.VMEM((rows, GDN_W), F32),
            pltpu.VMEM((rows, LANES), F32),
            pltpu.VMEM((rows, LANES), F32),
            pltpu.VMEM((n_mats // 2, 2 * c, 2 * c), BF16),
            pltpu.VMEM((n_mats // 2, 2 * c, 2 * c), BF16),
            pltpu.VMEM((n_mats, GDN_DK, c), BF16),
            pltpu.VMEM((rows, GDN_W), F32),
            pltpu.VMEM((nb * GDN_HEADS, GDN_DK, GDN_DV), F32),
        ],
        compiler_params=_cparams(("arbitrary", "arbitrary")),
        name="gdn_branch",
    )(u3, u3, u3, u3, ba3, conv_w, gparams, norm_w, tril)


def _strided_rows(r, n, dil):
    return slice(None) if dil == 1 else pl.ds(r, n, stride=dil)


def _attprep_kernel(q_ref, k_ref, v_ref, c_ref, s1_ref, s2_ref, *rest):
    outs, (qs, ks, vs) = rest[:-3], rest[-3:]
    ts = PREP_TS
    cs, s1, s2 = c_ref[...], s1_ref[...], s2_ref[...]
    for p in range(ATT_W // LANES):
        sl = slice(p * LANES, (p + 1) * LANES)
        for src, dst, scale in ((q_ref, qs, LOG2E * ATT_HD ** -0.5), (k_ref, ks, None)):
            x = src[0, :, sl]
            r = x * cs + pltpu.roll(x, LANES - ROPE_DIM // 2, axis=1) * s1 + pltpu.roll(x, ROPE_DIM // 2, axis=1) * s2
            dst[p] = r if scale is None else r * scale
        vs[p] = v_ref[0, :, sl]
    for di, (_, dil) in enumerate(DIL_PATTERNS):
        n = ts // dil
        for r in range(dil):
            rows = _strided_rows(r, n, dil)
            for p in range(ATT_W // LANES):
                sl = slice(p * LANES, (p + 1) * LANES)
                for dst, src in zip(outs[3 * di:3 * di + 3], (qs, ks, vs)):
                    dst[0, r, :, sl] = src[p, rows, :].astype(BF16)


def _att_prep(u3, cos_t, sin_a, sin_b):
    b, s, _ = u3.shape
    ts = PREP_TS
    col = lambda j: (lambda bi, si: (bi, si, j))
    tab = pl.BlockSpec((ts, LANES), lambda bi, si: (si, 0))
    out_specs, out_shape = [], []
    for _, dil in DIL_PATTERNS:
        out_specs += [pl.BlockSpec((1, dil, ts // dil, ATT_W), lambda bi, si: (bi, 0, si, 0))] * 3
        out_shape += [jax.ShapeDtypeStruct((b, dil, s // dil, ATT_W), BF16)] * 3
    return pl.pallas_call(
        _attprep_kernel,
        grid=(b, s // ts),
        in_specs=[pl.BlockSpec((1, ts, ATT_W), col(COL_AQ)), pl.BlockSpec((1, ts, ATT_W), col(COL_AK)),
                  pl.BlockSpec((1, ts, ATT_W), col(COL_AV)), tab, tab, tab],
        out_specs=out_specs,
        out_shape=out_shape,
        scratch_shapes=[pltpu.VMEM((ATT_W // LANES, ts, LANES), F32)] * 3,
        compiler_params=_cparams(("arbitrary", "arbitrary")),
        name="att_prep",
    )(u3, u3, u3, cos_t, sin_a, sin_b)


def _att_kernel(q_ref, k_ref, v_ref, o_ref, lse_ref, kprev, vprev):
    blk = ATT_BLOCK
    n = pl.program_id(2)

    @pl.when(n == 0)
    def _():
        kprev[...] = jnp.zeros(kprev.shape, BF16)
        vprev[...] = jnp.zeros(vprev.shape, BF16)

    row = lax.broadcasted_iota(jnp.int32, (blk, 2 * blk), 0)
    col = lax.broadcasted_iota(jnp.int32, (blk, 2 * blk), 1)
    first = jnp.where(n > 0, 0, 4 * blk)
    ahead = col - row
    lane = lax.broadcasted_iota(jnp.int32, (blk, LANES), 1)
    low = lane < ATT_HD

    for j in range(ATT_STEP_BLOCKS):
        rows = slice(j * blk, (j + 1) * blk)
        if j == 0:
            k_before, v_before = kprev, vprev
            valid = jnp.where(col < blk, ahead - first, blk - ahead) >= 0
        else:
            k_before, v_before = k_ref.at[(j - 1) * blk:j * blk], v_ref.at[(j - 1) * blk:j * blk]
            valid = jnp.where(col < blk, ahead, blk - ahead) >= 0

        scores = []
        for p in range(ATT_W // LANES):
            sl = slice(p * LANES, (p + 1) * LANES)
            kcat = jnp.concatenate([k_before[:, sl], k_ref[rows, sl]], axis=0)
            qp = q_ref[rows, sl]
            for hh in range(2):
                hm = low if hh == 0 else jnp.logical_not(low)
                scores.append(_dot_nt(jnp.where(hm, qp, jnp.zeros_like(qp)), kcat))

        lse_tile = jnp.zeros((blk, LANES), F32)
        for p in range(ATT_W // LANES):
            sl = slice(p * LANES, (p + 1) * LANES)
            vcat = jnp.concatenate([v_before[:, sl], v_ref[rows, sl]], axis=0)
            o_pair = None
            for hh in range(2):
                sc = jnp.where(valid, scores[2 * p + hh], NEG_INF)
                m = jnp.max(sc, axis=-1, keepdims=True)
                pe = jnp.exp2(sc - m)
                l = jnp.sum(pe, axis=-1, keepdims=True)
                o = _dot(pe.astype(BF16), vcat) * (1.0 / l)
                o_pair = o if hh == 0 else jnp.where(low, o_pair, o)
                lse_tile = jnp.where(lane == 2 * p + hh, m * LN2 + jnp.log(l), lse_tile)
            o_ref[rows, sl] = o_pair
        lse_ref[rows, :] = lse_tile

    last = slice((ATT_STEP_BLOCKS - 1) * blk, ATT_STEP_BLOCKS * blk)
    kprev[...] = k_ref[last, :]
    vprev[...] = v_ref[last, :]


def _att_pattern(qd, kd, vd):
    b, dil, lr, _ = qd.shape
    blk = ATT_STEP_BLOCKS * ATT_BLOCK
    idx = lambda bi, r, n: (bi, r, n, 0)
    spec = pl.BlockSpec((None, None, blk, ATT_W), idx)
    return pl.pallas_call(
        _att_kernel,
        grid=(b, dil, lr // blk),
        in_specs=[spec, spec, spec],
        out_specs=[spec, pl.BlockSpec((None, None, blk, LANES), idx)],
        out_shape=[jax.ShapeDtypeStruct((b, dil, lr, ATT_W), F32),
                   jax.ShapeDtypeStruct((b, dil, lr, LANES), F32)],
        scratch_shapes=[pltpu.VMEM((ATT_BLOCK, ATT_W), BF16), pltpu.VMEM((ATT_BLOCK, ATT_W), BF16)],
        compiler_params=_cparams(("arbitrary", "arbitrary", "arbitrary")),
        name=f"att_dil{dil}",
    )(qd, kd, vd)


def _combine_patterns(o_refs, l_refs, g_ref, e_ref, ya, oscr, lscr):
    tm = OUTPROJ_TM
    for g, (o_ref, l_ref) in enumerate(zip(o_refs[1:], l_refs[1:])):
        dil = DIL_PATTERNS[g + 1][1]
        n = tm // dil
        for r in range(dil):
            rows = pl.ds(r, n, stride=dil)
            lscr[g, rows, :] = l_ref[r]
            for p in range(ATT_W // LANES):
                oscr[g, p, rows, :] = o_ref[r, :, p * LANES:(p + 1) * LANES]
    l1, l2, l3 = l_refs[0][0], lscr[0], lscr[1]
    mx = jnp.maximum(jnp.maximum(l1, l2), l3)
    e1, e2, e3 = jnp.exp(l1 - mx), jnp.exp(l2 - mx), jnp.exp(l3 - mx)
    inv = 1.0 / (e1 + e2 + e3)
    spread = e_ref[...]
    ws = []
    for w in (e1 * inv, e2 * inv, e3 * inv):
        hi = w.astype(BF16)
        lo = (w - hi.astype(F32)).astype(BF16)
        ws.append(_dot(jnp.concatenate([hi, lo], axis=1), spread))
    for p in range(ATT_W // LANES):
        sl = slice(p * LANES, (p + 1) * LANES)
        tiles = (o_refs[0][0, :, sl], oscr[0, p], oscr[1, p])
        acc = None
        for w, o in zip(ws, tiles):
            t = w[:, sl] * o
            acc = t if acc is None else acc + t
        ya[:, sl] = (acc * _silu(g_ref[:, sl])).astype(BF16)


def _outproj_kernel(yc_ref, yg_ref, o1_ref, o2_ref, o3_ref, l1_ref, l2_ref, l3_ref, g_ref, e_ref, x_ref,
                    wc_ref, wg_ref, wa_ref, nw_ref, o_ref, *rest, final):
    ya, oscr, lscr = rest[-3:]
    acc = _dot(yc_ref[...], wc_ref[...]) + _dot(yg_ref[...], wg_ref[...])
    _combine_patterns((o1_ref, o2_ref, o3_ref), (l1_ref, l2_ref, l3_ref), g_ref, e_ref, ya, oscr, lscr)
    xo = x_ref[...] + (acc + _dot(ya[...], wa_ref[...]))
    if final:
        o_ref[...] = _rms_norm(xo, nw_ref[...])
    else:
        o_ref[...] = xo
        rest[0][...] = _rms_norm(xo, nw_ref[...]).astype(BF16)


def _out_proj(y_conv, y_gdn, outs, lses, u2, x2, w_c, w_g, w_a, layer, next_norm_w, final):
    m = x2.shape[0]
    tm = OUTPROJ_TM
    tiles_per_seq = outs[0].shape[2] // tm
    row = lambda i: (i, 0)
    fixed = lambda i: (0, 0)
    of_layer = lambda i: (layer, 0, 0)
    of_seq = lambda i: (i // tiles_per_seq, 0, i % tiles_per_seq, 0)
    ospecs = [pl.BlockSpec((None, dil, tm // dil, ATT_W), of_seq) for _, dil in DIL_PATTERNS]
    lspecs = [pl.BlockSpec((None, dil, tm // dil, LANES), of_seq) for _, dil in DIL_PATTERNS]
    head_of_lane = lax.broadcasted_iota(jnp.int32, (2 * LANES, ATT_W), 1) // ATT_HD
    spread = (head_of_lane == lax.broadcasted_iota(jnp.int32, (2 * LANES, ATT_W), 0) % LANES).astype(BF16)
    out_specs = [pl.BlockSpec((tm, D_MODEL), row)]
    out_shape = [jax.ShapeDtypeStruct((m, D_MODEL), F32)]
    if not final:
        out_specs.append(pl.BlockSpec((tm, D_MODEL), row))
        out_shape.append(jax.ShapeDtypeStruct((m, D_MODEL), BF16))
    return pl.pallas_call(
        functools.partial(_outproj_kernel, final=final),
        grid=(m // tm,),
        in_specs=[
            pl.BlockSpec((tm, CONV_CH), row),
            pl.BlockSpec((tm, GDN_W), row),
            *ospecs, *lspecs,
            pl.BlockSpec((tm, ATT_W), lambda i: (i, COL_AG)),
            pl.BlockSpec((2 * LANES, ATT_W), fixed),
            pl.BlockSpec((tm, D_MODEL), row),
            pl.BlockSpec((None, CONV_CH, D_MODEL), of_layer),
            pl.BlockSpec((None, GDN_W, D_MODEL), of_layer),
            pl.BlockSpec((None, ATT_W, D_MODEL), of_layer),
            pl.BlockSpec((1, D_MODEL), fixed),
        ],
        out_specs=out_specs,
        out_shape=out_shape,
        scratch_shapes=[pltpu.VMEM((tm, ATT_W), BF16),
                        pltpu.VMEM((2, ATT_W // LANES, tm, LANES), F32),
                        pltpu.VMEM((2, tm, LANES), F32)],
        compiler_params=_cparams(("arbitrary",)),
        name="out_proj_final" if final else "out_proj",
    )(y_conv, y_gdn, *outs, *lses, u2, spread, x2, w_c, w_g, w_a, next_norm_w)


def _rope_lane_tables(s):
    half = ROPE_DIM // 2
    inv = ROPE_THETA ** (-jnp.arange(half, dtype=F32) / half)
    ang = jnp.arange(s, dtype=F32)[:, None] * inv[None, :]
    cos, sin = jnp.cos(ang), jnp.sin(ang)
    ones = jnp.ones((s, ATT_HD - ROPE_DIM), F32)
    zeros = jnp.zeros((s, ATT_HD - half), F32)
    cos_h = jnp.concatenate([cos, cos, ones], axis=-1)
    sin_a = jnp.concatenate([-sin, zeros], axis=-1)
    sin_b = jnp.concatenate([jnp.zeros((s, half), F32), sin, zeros[:, half:]], axis=-1)
    rep = lambda t: jnp.concatenate([t, t], axis=-1)
    return rep(cos_h), rep(sin_a), rep(sin_b)


def kernel(x, norm_w, w_in, conv_qkv_w, a_log, dt_bias, gdn_norm_w, conf_dw_w, conf_dw_b, conf_ln_w,
           conf_ln_b, conf_pw_w, w_out, final_norm_w):
    b, s, d = x.shape
    depth = w_in.shape[0]
    m = b * s
    assert d == D_MODEL and m % INPROJ_TM == 0 and s % (DIL_PATTERNS[-1][1] * ATT_BLOCK * ATT_STEP_BLOCKS) == 0

    ba0 = 2 * CONV_CH + CONV_CH + 4 * GDN_W
    w_t = jnp.transpose(w_in, (0, 2, 1))
    w_main = jnp.concatenate([w_t[:, :ba0], w_t[:, ba0 + 2 * GDN_HEADS:]], axis=1).astype(BF16)
    w_ba = jnp.zeros((depth, BA_W, d), F32)
    w_ba = w_ba.at[:, 0:GDN_HEADS].set(w_t[:, ba0:ba0 + GDN_HEADS])
    w_ba = w_ba.at[:, LANES:LANES + GDN_HEADS].set(w_t[:, ba0 + GDN_HEADS:ba0 + 2 * GDN_HEADS])
    w_ba = w_ba.astype(BF16)
    w_out_b = w_out.astype(BF16)
    w_out_c, w_out_g, w_out_a = (w_out_b[:, :CONV_CH], w_out_b[:, CONV_CH:CONV_CH + GDN_W],
                                 w_out_b[:, CONV_CH + GDN_W:])
    pw_b = conf_pw_w.astype(BF16)
    dw_w = jnp.pad(conf_dw_w, ((0, 0), (0, CONV_TAPS_PAD - CONV_WIDTH), (0, 0)))
    dw_w = jnp.broadcast_to(dw_w[:, :, None, :], (depth, CONV_TAPS_PAD, SUBLANES, CONV_CH))
    cw = jnp.pad(conv_qkv_w, ((0, 0), (0, SUBLANES - SHORT_CONV), (0, 0)))
    gparams = jnp.zeros((depth, SUBLANES, LANES), F32)
    gparams = gparams.at[:, 0, 0:GDN_HEADS].set(a_log).at[:, 1, 0:GDN_HEADS].set(dt_bias)
    cos_t, sin_a, sin_b = _rope_lane_tables(s)

    x2 = x.reshape(m, d)
    xn = _pre_norm(x2, norm_w[0][None, :])
    for l in range(depth):
        last = l == depth - 1
        u, ba = _in_proj(xn, w_main, w_ba, l)
        u3 = u.reshape(b, s, MAIN_W)
        y_conv = _conv_branch(u3, dw_w[l], conf_dw_b[l][None, :], conf_ln_w[l][None, :], conf_ln_b[l][None, :],
                              pw_b[l])
        y_gdn = _gdn_branch(u3, ba.reshape(b, s, BA_W), cw[l], gparams[l], gdn_norm_w[l][None, :])
        qkv = _att_prep(u3, cos_t, sin_a, sin_b)
        outs, lses = [], []
        for g in range(len(DIL_PATTERNS)):
            o_g, lse_g = _att_pattern(*qkv[3 * g:3 * g + 3])
            outs.append(o_g)
            lses.append(lse_g)
        next_w = final_norm_w if last else norm_w[l + 1]
        res = _out_proj(y_conv.reshape(m, CONV_CH), y_gdn.reshape(m, GDN_W), outs, lses, u, x2,
                        w_out_c, w_out_g, w_out_a, l, next_w[None, :], final=last)
        x2 = res[0]
        xn = None if last else res[1]
    return x2.reshape(b, s, d)
```

```python
import functools

import jax
import jax.numpy as jnp
from jax import lax
from jax.experimental import pallas as pl
from jax.experimental.pallas import tpu as pltpu

F32 = jnp.float32
BF16 = jnp.bfloat16

D_MODEL = 2048
GDN_DK = 128
GDN_DV = 128
GDN_W = 768
GDN_HEADS = 6
ATT_HD = 64
ATT_W = 768
ATT_HEADS = 12
CONV_CH = 512
CONV_WIDTH = 31
SHORT_CONV = 4
GDN_CHUNK = 64
ROPE_THETA = 500000.0
ROPE_DIM = 16
DIL_PATTERNS = ((128, 1), (512, 4), (2048, 16))
ATT_BLOCK = 128
NEG_INF = -1e30
LOG2E = 1.4426950408889634
LN2 = 0.6931471805599453
MAIN_W = 2 * CONV_CH + CONV_CH + 4 * GDN_W + 4 * ATT_W
BA_W = 256

COL_GQ, COL_GK, COL_GV, COL_GZ, COL_AQ, COL_AK, COL_AV, COL_AG = 2, 3, 4, 5, 6, 7, 8, 9

LANES = 128
SUBLANES = 8
VMEM_LIMIT = 60 * 1024 * 1024
INPROJ_TM, INPROJ_TN = 2048, 768
OUTPROJ_TM = 512
CONV_TS, CONV_RC, CONV_HALO = 256, 32, 32
CONV_TAPS_PAD = 32
GDN_NB, GDN_TS = 4, 128
GDN_HALO = 16
GDN_TGROUP = 4
ATT_STEP_BLOCKS = 8
PREP_TS = 512


def _cparams(sem):
    return pltpu.CompilerParams(dimension_semantics=sem, vmem_limit_bytes=VMEM_LIMIT)


def _sigmoid(x):
    return 1.0 / (1.0 + jnp.exp(-x))


def _silu(x):
    return x * _sigmoid(x)


def _dot(a, b):
    return jnp.dot(a, b, preferred_element_type=F32)


def _dot_nt(a, b):
    return lax.dot_general(a, b, (((1,), (1,)), ((), ())), preferred_element_type=F32)


def _rms_norm(x, w):
    ms = jnp.mean(x * x, axis=-1, keepdims=True)
    return (x * lax.rsqrt(ms + 1e-6)) * w


def _prenorm_kernel(x_ref, nw_ref, xn_ref):
    xn_ref[...] = _rms_norm(x_ref[...], nw_ref[...]).astype(BF16)


def _pre_norm(x2, norm_w):
    m = x2.shape[0]
    tm = OUTPROJ_TM
    return pl.pallas_call(
        _prenorm_kernel,
        grid=(m // tm,),
        in_specs=[pl.BlockSpec((tm, D_MODEL), lambda i: (i, 0)), pl.BlockSpec((1, D_MODEL), lambda i: (0, 0))],
        out_specs=pl.BlockSpec((tm, D_MODEL), lambda i: (i, 0)),
        out_shape=jax.ShapeDtypeStruct((m, D_MODEL), BF16),
        compiler_params=_cparams(("arbitrary",)),
        name="pre_norm",
    )(x2, norm_w)


def _inproj_kernel(xn_ref, w_ref, wba_ref, u_ref, ba_ref):
    @pl.when(pl.program_id(1) == 0)
    def _():
        ba_ref[...] = _dot_nt(xn_ref[...], wba_ref[...])

    u_ref[...] = _dot_nt(xn_ref[...], w_ref[...])


def _in_proj(xn, w_main, w_ba, layer):
    m = xn.shape[0]
    tm, tn = INPROJ_TM, INPROJ_TN
    return pl.pallas_call(
        _inproj_kernel,
        grid=(m // tm, MAIN_W // tn),
        in_specs=[
            pl.BlockSpec((tm, D_MODEL), lambda i, j: (i, 0)),
            pl.BlockSpec((None, tn, D_MODEL), lambda i, j: (layer, j, 0)),
            pl.BlockSpec((None, BA_W, D_MODEL), lambda i, j: (layer, 0, 0)),
        ],
        out_specs=[
            pl.BlockSpec((tm, tn), lambda i, j: (i, j)),
            pl.BlockSpec((tm, BA_W), lambda i, j: (i, 0)),
        ],
        out_shape=[jax.ShapeDtypeStruct((m, MAIN_W), F32), jax.ShapeDtypeStruct((m, BA_W), F32)],
        compiler_params=_cparams(("arbitrary", "arbitrary")),
        name="in_proj",
    )(xn, w_main, w_ba)


def _conv_kernel(cin_ref, cgate_ref, dww_ref, dwb_ref, lnw_ref, lnb_ref, pw_ref, y_ref, hbuf, hsh, act):
    ts, rc, halo = CONV_TS, CONV_RC, CONV_HALO
    s = pl.program_id(1)

    @pl.when(s == 0)
    def _():
        hbuf[0:halo, :] = jnp.zeros((halo, CONV_CH), F32)

    @pl.when(s > 0)
    def _():
        hbuf[0:halo, :] = hbuf[ts:ts + halo, :]

    u = cin_ref[0]
    hbuf[halo:halo + ts, :] = u[:, :CONV_CH] * _sigmoid(u[:, CONV_CH:])

    for r in range(SUBLANES):
        span = ts + halo - (SUBLANES if r else 0)
        hsh[r, 0:span // SUBLANES] = hbuf[r:r + span, :].reshape(span // SUBLANES, SUBLANES, CONV_CH)

    base = halo - (CONV_WIDTH - 1)
    groups = rc // SUBLANES

    for c in range(ts // rc):
        r0 = c * rc
        acc = jnp.zeros((groups, SUBLANES, CONV_CH), F32) + dwb_ref[...]
        for k in range(CONV_WIDTH):
            shift = (base + k) % SUBLANES
            g0 = (r0 + base + k - shift) // SUBLANES
            acc = acc + dww_ref[k] * hsh[shift, g0:g0 + groups]
        mu = jnp.mean(acc, axis=-1, keepdims=True)
        d = acc - mu
        var = jnp.mean(d * d, axis=-1, keepdims=True)
        hn = d * lax.rsqrt(var + 1e-5) * lnw_ref[...] + lnb_ref[...]
        act[r0:r0 + rc, :] = _silu(hn).reshape(rc, CONV_CH).astype(BF16)

    y = _dot(act[...], pw_ref[...]) * _silu(cgate_ref[0])
    y_ref[0] = y.astype(BF16)


def _conv_branch(u3, dw_w, dw_b, ln_w, ln_b, pw_w):
    b, s, _ = u3.shape
    ts = CONV_TS
    fixed = lambda bi, si: (0, 0)
    return pl.pallas_call(
        _conv_kernel,
        grid=(b, s // ts),
        in_specs=[
            pl.BlockSpec((1, ts, 2 * CONV_CH), lambda bi, si: (bi, si, 0)),
            pl.BlockSpec((1, ts, CONV_CH), lambda bi, si: (bi, si, 2)),
            pl.BlockSpec((CONV_TAPS_PAD, SUBLANES, CONV_CH), lambda bi, si: (0, 0, 0)),
            pl.BlockSpec((1, CONV_CH), fixed),
            pl.BlockSpec((1, CONV_CH), fixed),
            pl.BlockSpec((1, CONV_CH), fixed),
            pl.BlockSpec((CONV_CH, CONV_CH), fixed),
        ],
        out_specs=pl.BlockSpec((1, ts, CONV_CH), lambda bi, si: (bi, si, 0)),
        out_shape=jax.ShapeDtypeStruct((b, s, CONV_CH), BF16),
        scratch_shapes=[pltpu.VMEM((ts + CONV_HALO, CONV_CH), F32),
                        pltpu.VMEM((SUBLANES, (ts + CONV_HALO) // SUBLANES, SUBLANES, CONV_CH), F32),
                        pltpu.VMEM((ts, CONV_CH), BF16)],
        compiler_params=_cparams(("arbitrary", "arbitrary")),
        name="conv_branch",
    )(u3, u3, dw_w, dw_b, ln_w, ln_b, pw_w)


def _split3(x):
    hi = x.astype(BF16)
    r1 = x - hi.astype(F32)
    mid = r1.astype(BF16)
    lo = (r1 - mid.astype(F32)).astype(BF16)
    return hi, mid, lo


def _blockdiag(p, low):
    z = jnp.zeros_like(p)
    return jnp.concatenate([jnp.where(low, p, z), jnp.where(low, z, p)], axis=0)


def _unit_lower_inverse_pairs(a_list, ii, jj, low):
    c = a_list[0].shape[0]
    same = lambda log2: jnp.right_shift(ii, log2) == jnp.right_shift(jj, log2)
    in_blk = same(1)
    rs = [-jnp.where(in_blk, a, 0.0) for a in a_list]
    lg = 1
    while (1 << lg) < c:
        in_pair = same(lg + 1)
        es = [jnp.where(in_pair, jnp.where(in_blk, 0.0, a), 0.0) for a in a_list]
        rbs = [r.astype(BF16) for r in rs]
        xs = [_dot(rb, _blockdiag(e.astype(BF16), low)) for rb, e in zip(rbs, es)]
        ys = [_dot((e + x).astype(BF16), _blockdiag(rb, low)) for e, x, rb in zip(es, xs, rbs)]
        rs = [r - e - x - y for r, e, x, y in zip(rs, es, xs, ys)]
        in_blk = in_pair
        lg += 1
    return rs


def _gdn_kernel(q_ref, k_ref, v_ref, z_ref, ba_ref, cw_ref, gp_ref, nw_ref, tril_ref, y_ref,
                carry, xbuf, qn, kn, vn, gcum, beta, tbuf, qkbuf, kdbuf, obuf, state):
    nb, ts, c = GDN_NB, GDN_TS, GDN_CHUNK
    n_chunks = ts // c
    w3 = 3 * GDN_W
    heads = range(GDN_HEADS)
    hsl = lambda h: slice(h * GDN_DK, (h + 1) * GDN_DK)
    s = pl.program_id(1)

    @pl.when(s == 0)
    def _():
        carry[...] = jnp.zeros(carry.shape, F32)
        state[...] = jnp.zeros(state.shape, F32)

    assert SHORT_CONV == 4
    hal = GDN_HALO
    gsteps = []
    for bb in range(nb):
        rows_b = slice(bb * ts, (bb + 1) * ts)
        xbuf[bb, 0:hal, :] = carry[bb]
        xbuf[bb, hal:hal + ts, 0:GDN_W] = q_ref[bb]
        xbuf[bb, hal:hal + ts, GDN_W:2 * GDN_W] = k_ref[bb]
        xbuf[bb, hal:hal + ts, 2 * GDN_W:w3] = v_ref[bb]
        carry[bb] = xbuf[bb, ts:ts + hal, :]
        for part, dst in ((0, qn), (1, kn), (2, vn)):
            for h in heads:
                lo = part * GDN_W + h * GDN_DK
                cols = slice(lo, lo + GDN_DK)
                w0, w1, w2, w3_ = (cw_ref[t:t + 1, cols] for t in range(SHORT_CONV))
                x_now = xbuf[bb, hal - SUBLANES:hal + ts, cols]
                x_m2 = xbuf[bb, hal - SUBLANES - 2:hal + ts - 2, cols]
                q_ext = w2 * x_now + w0 * x_m2
                acc = (w3_ * x_now[SUBLANES:] + w1 * x_m2[SUBLANES:]) + q_ext[SUBLANES - 1:SUBLANES - 1 + ts]
                acc = _silu(acc)
                if part == 0:
                    acc = (acc * lax.rsqrt(jnp.sum(acc * acc, axis=-1, keepdims=True) + 1e-6)) * (GDN_DK ** -0.5)
                elif part == 1:
                    acc = acc * lax.rsqrt(jnp.sum(acc * acc, axis=-1, keepdims=True) + 1e-6)
                dst[rows_b, hsl(h)] = acc

        ba = ba_ref[bb]
        beta[rows_b, :] = _sigmoid(ba[:, 0:LANES])
        xa = ba[:, LANES:2 * LANES] + gp_ref[1:2, :]
        softplus = jnp.maximum(xa, 0.0) + jnp.log(1.0 + jnp.exp(-jnp.abs(xa)))
        gsteps.append(-jnp.exp(gp_ref[0:1, :]) * softplus)

    tril = tril_ref[...]
    g_hi, g_mid, g_lo = _split3(jnp.concatenate(gsteps, axis=0))
    gcum[...] = _dot(tril, g_hi) + _dot(tril, g_mid) + _dot(tril, g_lo)

    ii = lax.broadcasted_iota(jnp.int32, (c, 2 * c), 0)
    lane = lax.broadcasted_iota(jnp.int32, (c, 2 * c), 1)
    jj = jnp.bitwise_and(lane, c - 1)
    low = lane < c
    low_row = low[0:1, :]
    causal = ii >= jj
    strict = ii > jj
    eye = jnp.where(ii == jj, 1.0, 0.0)
    pairs = range(GDN_HEADS // 2)

    def t_body(gi, _):
        items = []
        for cc in range(GDN_TGROUP):
            unit = gi * GDN_TGROUP + cc
            rows_u = pl.ds(pl.multiple_of(unit * c, c), c)
            g_blk = gcum[rows_u, :]
            g_t = g_blk.T
            g_t2 = jnp.concatenate([g_t, g_t], axis=1)
            b_blk = beta[rows_u, :]
            for h in heads:
                gr = g_t[h:h + 1, :]
                kdbuf[unit * GDN_HEADS + h] = (kn[rows_u, hsl(h)].T * jnp.exp(gr[:, c - 1:c] - gr)).astype(BF16)
            for p in pairs:
                ha, hb = 2 * p, 2 * p + 1
                gc = jnp.where(low, g_blk[:, ha:ha + 1], g_blk[:, hb:hb + 1])
                gr = jnp.where(low_row, g_t2[ha:ha + 1, :], g_t2[hb:hb + 1, :])
                bc = jnp.where(low, b_blk[:, ha:ha + 1], b_blk[:, hb:hb + 1])
                items.append((unit * (GDN_HEADS // 2) + p, rows_u, ha, hb, gc - gr, bc))
        zero = jnp.zeros((c, GDN_DK), BF16)
        decays, kks, qks = [], [], []
        for _, rows_u, ha, hb, gdiff, _ in items:
            ka, kb = kn[rows_u, hsl(ha)].astype(BF16), kn[rows_u, hsl(hb)].astype(BF16)
            qa, qb = qn[rows_u, hsl(ha)].astype(BF16), qn[rows_u, hsl(hb)].astype(BF16)
            k_diag = jnp.concatenate([jnp.concatenate([ka, zero], axis=1), jnp.concatenate([zero, kb], axis=1)], axis=0)
            decays.append(jnp.where(causal, jnp.exp(jnp.where(causal, gdiff, 0.0)), 0.0))
            kks.append(_dot_nt(jnp.concatenate([ka, kb], axis=1), k_diag))
            qks.append(_dot_nt(jnp.concatenate([qa, qb], axis=1), k_diag))
        a_s = [jnp.where(strict, it[5] * kk * dc, 0.0) for it, kk, dc in zip(items, kks, decays)]
        for it, qk, dc in zip(items, qks, decays):
            qkbuf[it[0]] = _blockdiag(jnp.where(causal, qk * dc, 0.0).astype(BF16), low)
        rs = _unit_lower_inverse_pairs(a_s, ii, jj, low)
        for it, r in zip(items, rs):
            tbuf[it[0]] = _blockdiag((r + eye).astype(BF16), low)
        return 0

    lax.fori_loop(0, nb * n_chunks // GDN_TGROUP, t_body, 0)

    chains = [(bb, h) for bb in range(nb) for h in heads]

    def chunk_body(ci_, _):
        rows, mats, pair_mats, gcs, bcs = [], [], [], [], []
        for bb in range(nb):
            unit = bb * n_chunks + ci_
            rows_u = pl.ds(pl.multiple_of(unit * c, c), c)
            g_blk = gcum[rows_u, :]
            b_blk = beta[rows_u, :]
            pair_mats += [unit * (GDN_HEADS // 2) + p for p in pairs]
            for h in heads:
                rows.append(rows_u)
                mats.append(unit * GDN_HEADS + h)
                gcs.append(g_blk[:, h:h + 1])
                bcs.append(b_blk[:, h:h + 1])
        n = range(len(chains))
        egs = [jnp.exp(gc) for gc in gcs]
        glasts = [gc[c - 1:c, :] for gc in gcs]
        s_hs = [state[bb * GDN_HEADS + h] for bb, h in chains]
        s_bs = [sh.astype(BF16) for sh in s_hs]
        kq_s = [_dot(jnp.concatenate([kn[rows[i], hsl(chains[i][1])].astype(BF16),
                                      (qn[rows[i], hsl(chains[i][1])] * egs[i]).astype(BF16)], axis=0), s_bs[i]) for i in n]
        o_state = [r[c:2 * c, :] for r in kq_s]
        us = [bcs[i] * (vn[rows[i], hsl(chains[i][1])] - egs[i] * kq_s[i][0:c, :]) for i in n]
        half = range(len(chains) // 2)
        v_pairs = [_dot(tbuf[pair_mats[j]],
                        jnp.concatenate([us[2 * j], us[2 * j + 1]], axis=0).astype(BF16)).astype(BF16) for j in half]
        vbs = [v_pairs[i // 2][(i % 2) * c:(i % 2 + 1) * c, :] for i in n]
        for i, (bb, h) in enumerate(chains):
            state[bb * GDN_HEADS + h] = s_hs[i] * jnp.exp(glasts[i]) + _dot(kdbuf[mats[i]], vbs[i])
        o_pairs = [_dot(qkbuf[pair_mats[j]], v_pairs[j]) for j in half]
        for i, (bb, h) in enumerate(chains):
            obuf[rows[i], hsl(h)] = o_state[i] + o_pairs[i // 2][(i % 2) * c:(i % 2 + 1) * c, :]
        return 0

    lax.fori_loop(0, n_chunks, chunk_body, 0)

    for bb in range(nb):
        for h in heads:
            o = obuf[bb * ts:(bb + 1) * ts, hsl(h)]
            on = (o * lax.rsqrt(jnp.mean(o * o, axis=-1, keepdims=True) + 1e-6)) * nw_ref[...]
            y_ref[bb, :, hsl(h)] = (on * _silu(z_ref[bb, :, hsl(h)])).astype(BF16)


def _gdn_branch(u3, ba3, conv_w, gparams, norm_w):
    b, s, _ = u3.shape
    nb, ts, c = GDN_NB, GDN_TS, GDN_CHUNK
    assert b % nb == 0 and (nb * ts // c) % GDN_TGROUP == 0
    fixed = lambda bi, si: (0, 0)
    col = lambda j: (lambda bi, si: (bi, si, j))
    rows = nb * ts
    ri = lax.broadcasted_iota(jnp.int32, (rows, rows), 0)
    ci = lax.broadcasted_iota(jnp.int32, (rows, rows), 1)
    tril = ((ri // c == ci // c) & (ri >= ci)).astype(BF16)
    n_mats = (rows // c) * GDN_HEADS
    return pl.pallas_call(
        _gdn_kernel,
        grid=(b // nb, s // ts),
        in_specs=[
            pl.BlockSpec((nb, ts, GDN_W), col(COL_GQ)),
            pl.BlockSpec((nb, ts, GDN_W), col(COL_GK)),
            pl.BlockSpec((nb, ts, GDN_W), col(COL_GV)),
            pl.BlockSpec((nb, ts, GDN_W), col(COL_GZ)),
            pl.BlockSpec((nb, ts, BA_W), col(0)),
            pl.BlockSpec((SUBLANES, 3 * GDN_W), fixed),
            pl.BlockSpec((SUBLANES, LANES), fixed),
            pl.BlockSpec((1, GDN_DV), fixed),
            pl.BlockSpec((rows, rows), fixed),
        ],
        out_specs=pl.BlockSpec((nb, ts, GDN_W), col(0)),
        out_shape=jax.ShapeDtypeStruct((b, s, GDN_W), BF16),
        scratch_shapes=[
            pltpu.VMEM((nb, GDN_HALO, 3 * GDN_W), F32),
            pltpu.VMEM((nb, ts + GDN_HALO, 3 * GDN_W), F32),
            pltpu.VMEM((rows, GDN_W), F32),
            pltpu.VMEM((rows, GDN_W), F32),
            pltpu.VMEM((rows, GDN_W), F32),
            pltpu.VMEM((rows, LANES), F32),
            pltpu.VMEM((rows, LANES), F32),
            pltpu.VMEM((n_mats // 2, 2 * c, 2 * c), BF16),
            pltpu.VMEM((n_mats // 2, 2 * c, 2 * c), BF16),
            pltpu.VMEM((n_mats, GDN_DK, c), BF16),
            pltpu.VMEM((rows, GDN_W), F32),
            pltpu.VMEM((nb * GDN_HEADS, GDN_DK, GDN_DV), F32),
        ],
        compiler_params=_cparams(("arbitrary", "arbitrary")),
        name="gdn_branch",
    )(u3, u3, u3, u3, ba3, conv_w, gparams, norm_w, tril)


def _strided_rows(r, n, dil):
    return slice(None) if dil == 1 else pl.ds(r, n, stride=dil)


def _attprep_kernel(q_ref, k_ref, v_ref, c_ref, s1_ref, s2_ref, *rest):
    outs, (qs, ks, vs) = rest[:-3], rest[-3:]
    ts = PREP_TS
    cs, s1, s2 = c_ref[...], s1_ref[...], s2_ref[...]
    for p in range(ATT_W // LANES):
        sl = slice(p * LANES, (p + 1) * LANES)
        for src, dst, scale in ((q_ref, qs, LOG2E * ATT_HD ** -0.5), (k_ref, ks, None)):
            x = src[0, :, sl]
            r = x * cs + pltpu.roll(x, LANES - ROPE_DIM // 2, axis=1) * s1 + pltpu.roll(x, ROPE_DIM // 2, axis=1) * s2
            dst[p] = r if scale is None else r * scale
        vs[p] = v_ref[0, :, sl]
    for di, (_, dil) in enumerate(DIL_PATTERNS):
        n = ts // dil
        for r in range(dil):
            rows = _strided_rows(r, n, dil)
            for p in range(ATT_W // LANES):
                sl = slice(p * LANES, (p + 1) * LANES)
                for dst, src in zip(outs[3 * di:3 * di + 3], (qs, ks, vs)):
                    dst[0, r, :, sl] = src[p, rows, :].astype(BF16)


def _att_prep(u3, cos_t, sin_a, sin_b):
    b, s, _ = u3.shape
    ts = PREP_TS
    col = lambda j: (lambda bi, si: (bi, si, j))
    tab = pl.BlockSpec((ts, LANES), lambda bi, si: (si, 0))
    out_specs, out_shape = [], []
    for _, dil in DIL_PATTERNS:
        out_specs += [pl.BlockSpec((1, dil, ts // dil, ATT_W), lambda bi, si: (bi, 0, si, 0))] * 3
        out_shape += [jax.ShapeDtypeStruct((b, dil, s // dil, ATT_W), BF16)] * 3
    return pl.pallas_call(
        _attprep_kernel,
        grid=(b, s // ts),
        in_specs=[pl.BlockSpec((1, ts, ATT_W), col(COL_AQ)), pl.BlockSpec((1, ts, ATT_W), col(COL_AK)),
                  pl.BlockSpec((1, ts, ATT_W), col(COL_AV)), tab, tab, tab],
        out_specs=out_specs,
        out_shape=out_shape,
        scratch_shapes=[pltpu.VMEM((ATT_W // LANES, ts, LANES), F32)] * 3,
        compiler_params=_cparams(("arbitrary", "arbitrary")),
        name="att_prep",
    )(u3, u3, u3, cos_t, sin_a, sin_b)


def _att_kernel(q_ref, k_ref, v_ref, o_ref, lse_ref, kprev, vprev, *, n_blocks):
    blk = ATT_BLOCK
    n = pl.program_id(2)

    @pl.when(n == 0)
    def _():
        kprev[...] = jnp.zeros(kprev.shape, BF16)
        vprev[...] = jnp.zeros(vprev.shape, BF16)

    row = lax.broadcasted_iota(jnp.int32, (blk, 2 * blk), 0)
    col = lax.broadcasted_iota(jnp.int32, (blk, 2 * blk), 1)
    first = jnp.where(n > 0, 0, 4 * blk)
    ahead = col - row
    lane = lax.broadcasted_iota(jnp.int32, (blk, LANES), 1)
    low = lane < ATT_HD

    for j in range(n_blocks):
        rows = slice(j * blk, (j + 1) * blk)
        if j == 0:
            k_before, v_before = kprev, vprev
            valid = jnp.where(col < blk, ahead - first, blk - ahead) >= 0
        else:
            k_before, v_before = k_ref.at[(j - 1) * blk:j * blk], v_ref.at[(j - 1) * blk:j * blk]
            valid = jnp.where(col < blk, ahead, blk - ahead) >= 0

        scores = []
        for p in range(ATT_W // LANES):
            sl = slice(p * LANES, (p + 1) * LANES)
            kcat = jnp.concatenate([k_before[:, sl], k_ref[rows, sl]], axis=0)
            qp = q_ref[rows, sl]
            for hh in range(2):
                hm = low if hh == 0 else jnp.logical_not(low)
                scores.append(_dot_nt(jnp.where(hm, qp, jnp.zeros_like(qp)), kcat))

        lse_tile = jnp.zeros((blk, LANES), F32)
        for p in range(ATT_W // LANES):
            sl = slice(p * LANES, (p + 1) * LANES)
            vcat = jnp.concatenate([v_before[:, sl], v_ref[rows, sl]], axis=0)
            o_pair = None
            for hh in range(2):
                sc = jnp.where(valid, scores[2 * p + hh], NEG_INF)
                m = jnp.max(sc, axis=-1, keepdims=True)
                pe = jnp.exp2(sc - m)
                l = jnp.sum(pe, axis=-1, keepdims=True)
                o = _dot(pe.astype(BF16), vcat) * (1.0 / l)
                o_pair = o if hh == 0 else jnp.where(low, o_pair, o)
                lse_tile = jnp.where(lane == 2 * p + hh, m * LN2 + jnp.log(l), lse_tile)
            o_ref[rows, sl] = o_pair
        lse_ref[rows, :] = lse_tile

    last = slice((n_blocks - 1) * blk, n_blocks * blk)
    kprev[...] = k_ref[last, :]
    vprev[...] = v_ref[last, :]


def _att_pattern(qd, kd, vd):
    b, dil, lr, _ = qd.shape
    n_blocks = min(ATT_STEP_BLOCKS, lr // ATT_BLOCK)
    blk = n_blocks * ATT_BLOCK
    idx = lambda bi, r, n: (bi, r, n, 0)
    spec = pl.BlockSpec((None, None, blk, ATT_W), idx)
    return pl.pallas_call(
        functools.partial(_att_kernel, n_blocks=n_blocks),
        grid=(b, dil, lr // blk),
        in_specs=[spec, spec, spec],
        out_specs=[spec, pl.BlockSpec((None, None, blk, LANES), idx)],
        out_shape=[jax.ShapeDtypeStruct((b, dil, lr, ATT_W), F32),
                   jax.ShapeDtypeStruct((b, dil, lr, LANES), F32)],
        scratch_shapes=[pltpu.VMEM((ATT_BLOCK, ATT_W), BF16), pltpu.VMEM((ATT_BLOCK, ATT_W), BF16)],
        compiler_params=_cparams(("arbitrary", "arbitrary", "arbitrary")),
        name=f"att_dil{dil}",
    )(qd, kd, vd)


def _combine_patterns(o_refs, l_refs, g_ref, e_ref, ya, oscr, lscr):
    tm = OUTPROJ_TM
    for g, (o_ref, l_ref) in enumerate(zip(o_refs[1:], l_refs[1:])):
        dil = DIL_PATTERNS[g + 1][1]
        n = tm // dil
        for r in range(dil):
            rows = pl.ds(r, n, stride=dil)
            lscr[g, rows, :] = l_ref[r]
            for p in range(ATT_W // LANES):
                oscr[g, p, rows, :] = o_ref[r, :, p * LANES:(p + 1) * LANES]
    l1, l2, l3 = l_refs[0][0], lscr[0], lscr[1]
    mx = jnp.maximum(jnp.maximum(l1, l2), l3)
    e1, e2, e3 = jnp.exp(l1 - mx), jnp.exp(l2 - mx), jnp.exp(l3 - mx)
    inv = 1.0 / (e1 + e2 + e3)
    spread = e_ref[...]
    ws = []
    for w in (e1 * inv, e2 * inv, e3 * inv):
        hi = w.astype(BF16)
        lo = (w - hi.astype(F32)).astype(BF16)
        ws.append(_dot(jnp.concatenate([hi, lo], axis=1), spread))
    for p in range(ATT_W // LANES):
        sl = slice(p * LANES, (p + 1) * LANES)
        tiles = (o_refs[0][0, :, sl], oscr[0, p], oscr[1, p])
        acc = None
        for w, o in zip(ws, tiles):
            t = w[:, sl] * o
            acc = t if acc is None else acc + t
        ya[:, sl] = (acc * _silu(g_ref[:, sl])).astype(BF16)


def _outproj_kernel(yc_ref, yg_ref, o1_ref, o2_ref, o3_ref, l1_ref, l2_ref, l3_ref, g_ref, e_ref, x_ref,
                    wc_ref, wg_ref, wa_ref, nw_ref, o_ref, *rest, final):
    ya, oscr, lscr = rest[-3:]
    acc = _dot(yc_ref[...], wc_ref[...]) + _dot(yg_ref[...], wg_ref[...])
    _combine_patterns((o1_ref, o2_ref, o3_ref), (l1_ref, l2_ref, l3_ref), g_ref, e_ref, ya, oscr, lscr)
    xo = x_ref[...] + (acc + _dot(ya[...], wa_ref[...]))
    if final:
        o_ref[...] = _rms_norm(xo, nw_ref[...])
    else:
        o_ref[...] = xo
        rest[0][...] = _rms_norm(xo, nw_ref[...]).astype(BF16)


def _out_proj(y_conv, y_gdn, outs, lses, u2, x2, w_c, w_g, w_a, layer, next_norm_w, final):
    m = x2.shape[0]
    tm = OUTPROJ_TM
    tiles_per_seq = outs[0].shape[2] // tm
    row = lambda i: (i, 0)
    fixed = lambda i: (0, 0)
    of_layer = lambda i: (layer, 0, 0)
    of_seq = lambda i: (i // tiles_per_seq, 0, i % tiles_per_seq, 0)
    ospecs = [pl.BlockSpec((None, dil, tm // dil, ATT_W), of_seq) for _, dil in DIL_PATTERNS]
    lspecs = [pl.BlockSpec((None, dil, tm // dil, LANES), of_seq) for _, dil in DIL_PATTERNS]
    head_of_lane = lax.broadcasted_iota(jnp.int32, (2 * LANES, ATT_W), 1) // ATT_HD
    spread = (head_of_lane == lax.broadcasted_iota(jnp.int32, (2 * LANES, ATT_W), 0) % LANES).astype(BF16)
    out_specs = [pl.BlockSpec((tm, D_MODEL), row)]
    out_shape = [jax.ShapeDtypeStruct((m, D_MODEL), F32)]
    if not final:
        out_specs.append(pl.BlockSpec((tm, D_MODEL), row))
        out_shape.append(jax.ShapeDtypeStruct((m, D_MODEL), BF16))
    return pl.pallas_call(
        functools.partial(_outproj_kernel, final=final),
        grid=(m // tm,),
        in_specs=[
            pl.BlockSpec((tm, CONV_CH), row),
            pl.BlockSpec((tm, GDN_W), row),
            *ospecs, *lspecs,
            pl.BlockSpec((tm, ATT_W), lambda i: (i, COL_AG)),
            pl.BlockSpec((2 * LANES, ATT_W), fixed),
            pl.BlockSpec((tm, D_MODEL), row),
            pl.BlockSpec((None, CONV_CH, D_MODEL), of_layer),
            pl.BlockSpec((None, GDN_W, D_MODEL), of_layer),
            pl.BlockSpec((None, ATT_W, D_MODEL), of_layer),
            pl.BlockSpec((1, D_MODEL), fixed),
        ],
        out_specs=out_specs,
        out_shape=out_shape,
        scratch_shapes=[pltpu.VMEM((tm, ATT_W), BF16),
                        pltpu.VMEM((2, ATT_W // LANES, tm, LANES), F32),
                        pltpu.VMEM((2, tm, LANES), F32)],
        compiler_params=_cparams(("arbitrary",)),
        name="out_proj_final" if final else "out_proj",
    )(y_conv, y_gdn, *outs, *lses, u2, spread, x2, w_c, w_g, w_a, next_norm_w)


def _rope_lane_tables(s):
    half = ROPE_DIM // 2
    inv = ROPE_THETA ** (-jnp.arange(half, dtype=F32) / half)
    ang = jnp.arange(s, dtype=F32)[:, None] * inv[None, :]
    cos, sin = jnp.cos(ang), jnp.sin(ang)
    ones = jnp.ones((s, ATT_HD - ROPE_DIM), F32)
    zeros = jnp.zeros((s, ATT_HD - half), F32)
    cos_h = jnp.concatenate([cos, cos, ones], axis=-1)
    sin_a = jnp.concatenate([-sin, zeros], axis=-1)
    sin_b = jnp.concatenate([jnp.zeros((s, half), F32), sin, zeros[:, half:]], axis=-1)
    rep = lambda t: jnp.concatenate([t, t], axis=-1)
    return rep(cos_h), rep(sin_a), rep(sin_b)


def kernel(x, norm_w, w_in, conv_qkv_w, a_log, dt_bias, gdn_norm_w, conf_dw_w, conf_dw_b, conf_ln_w,
           conf_ln_b, conf_pw_w, w_out, final_norm_w):
    b, s, d = x.shape
    depth = w_in.shape[0]
    m = b * s
    assert d == D_MODEL and m % INPROJ_TM == 0 and s % (DIL_PATTERNS[-1][1] * ATT_BLOCK) == 0 and s % (ATT_BLOCK * ATT_STEP_BLOCKS) == 0

    ba0 = 2 * CONV_CH + CONV_CH + 4 * GDN_W
    w_t = jnp.transpose(w_in, (0, 2, 1))
    w_main = jnp.concatenate([w_t[:, :ba0], w_t[:, ba0 + 2 * GDN_HEADS:]], axis=1).astype(BF16)
    ba_rows = jnp.transpose(w_in[:, :, ba0:ba0 + 2 * GDN_HEADS], (0, 2, 1)).astype(BF16)
    gap = jnp.zeros((depth, LANES - GDN_HEADS, d), BF16)
    w_ba = jnp.concatenate([ba_rows[:, :GDN_HEADS], gap, ba_rows[:, GDN_HEADS:], gap], axis=1)
    w_out_b = w_out.astype(BF16)
    w_out_c, w_out_g, w_out_a = (w_out_b[:, :CONV_CH], w_out_b[:, CONV_CH:CONV_CH + GDN_W],
                                 w_out_b[:, CONV_CH + GDN_W:])
    pw_b = conf_pw_w.astype(BF16)
    dw_w = jnp.pad(conf_dw_w, ((0, 0), (0, CONV_TAPS_PAD - CONV_WIDTH), (0, 0)))
    dw_w = jnp.broadcast_to(dw_w[:, :, None, :], (depth, CONV_TAPS_PAD, SUBLANES, CONV_CH))
    cw = jnp.pad(conv_qkv_w, ((0, 0), (0, SUBLANES - SHORT_CONV), (0, 0)))
    gparams = jnp.zeros((depth, SUBLANES, LANES), F32)
    gparams = gparams.at[:, 0, 0:GDN_HEADS].set(a_log).at[:, 1, 0:GDN_HEADS].set(dt_bias)
    cos_t, sin_a, sin_b = _rope_lane_tables(s)

    x2 = x.reshape(m, d)
    xn = _pre_norm(x2, norm_w[0][None, :])
    for l in range(depth):
        last = l == depth - 1
        u, ba = _in_proj(xn, w_main, w_ba, l)
        u3 = u.reshape(b, s, MAIN_W)
        y_conv = _conv_branch(u3, dw_w[l], conf_dw_b[l][None, :], conf_ln_w[l][None, :], conf_ln_b[l][None, :],
                              pw_b[l])
        y_gdn = _gdn_branch(u3, ba.reshape(b, s, BA_W), cw[l], gparams[l], gdn_norm_w[l][None, :])
        qkv = _att_prep(u3, cos_t, sin_a, sin_b)
        outs, lses = [], []
        for g in range(len(DIL_PATTERNS)):
            o_g, lse_g = _att_pattern(*qkv[3 * g:3 * g + 3])
            outs.append(o_g)
            lses.append(lse_g)
        next_w = final_norm_w if last else norm_w[l + 1]
        res = _out_proj(y_conv.reshape(m, CONV_CH), y_gdn.reshape(m, GDN_W), outs, lses, u, x2,
                        w_out_c, w_out_g, w_out_a, l, next_w[None, :], final=last)
        x2 = res[0]
        xn = None if last else res[1]
    return x2.reshape(b, s, d)
```
